```python
import math
import jax, jax.numpy as jnp
from jax import lax
import numpy as np

D_MODEL = 1024
BATCH = 8
SEQ = 2048
DEPTH = 4
DEC_BATCH = 32
DEC_SEQ = 8
PAST_LEN = 8192
PAGE_SIZE = 128

HEAD_DIM = 64
MIX_WIDTH = D_MODEL
NSA_HEADS = 8
NSA_KV_HEADS = 2
NSA_GROUP = NSA_HEADS // NSA_KV_HEADS
NSA_WIDTH = NSA_HEADS * HEAD_DIM
KV_W = NSA_KV_HEADS * HEAD_DIM
CONV_CH = MIX_WIDTH - NSA_WIDTH
N_BRANCH = 3
CMP_BLOCK = 32
CMP_STRIDE = 16
CMP_HIDDEN = 128
SEL_BLOCK = 64
N_SELECT = 16
WINDOW = 512
CONV_WIDTH = 31
D_FF = 2816
FFN_CONV_WIDTH = 3
ROPE_THETA = 10000.0
EPS = 1e-6
Q_BLOCK = 128
GATE_COLS = NSA_HEADS * N_BRANCH
IN_COLS = NSA_WIDTH + 6 * KV_W + GATE_COLS + 2 * CONV_CH

kernel_name = "hymba_nsa_conformer_convffn_step"


def rms_norm(x, g):
    xf = x.astype(jnp.float32)
    y = xf * lax.rsqrt(jnp.mean(xf * xf, -1, keepdims=True) + EPS)
    return (y * g.astype(jnp.float32)).astype(x.dtype)


def layer_norm(x, g, b):
    xf = x.astype(jnp.float32)
    mu = jnp.mean(xf, -1, keepdims=True)
    var = jnp.mean(jnp.square(xf - mu), -1, keepdims=True)
    y = (xf - mu) * lax.rsqrt(var + EPS) * g.astype(jnp.float32) + b.astype(jnp.float32)
    return y.astype(x.dtype)


def rope(x, pos):
    half = HEAD_DIM // 2
    inv = ROPE_THETA ** (-jnp.arange(half, dtype=jnp.float32) / half)
    ang = pos.astype(jnp.float32)[:, None] * inv[None, :]
    cos = jnp.cos(ang)[:, None, :]
    sin = jnp.sin(ang)[:, None, :]
    xf = x.astype(jnp.float32)
    x1, x2 = xf[..., :half], xf[..., half:]
    return jnp.concatenate([x1 * cos - x2 * sin, x2 * cos + x1 * sin], -1).astype(x.dtype)


def masked_softmax(s, mask):
    s = jnp.where(mask, s.astype(jnp.float32), -1e30)
    m = jnp.max(s, -1, keepdims=True)
    e = jnp.where(mask, jnp.exp(s - m), 0.0)
    return e / jnp.maximum(jnp.sum(e, -1, keepdims=True), 1e-30)


def dwconv_valid(xx, w):
    c = w.shape[1]
    return lax.conv_general_dilated(xx, w[:, None, :].astype(xx.dtype), (1,), "VALID",
                                    dimension_numbers=("NWC", "WIO", "NWC"), feature_group_count=c)


def in_projection(h, w_in, q_g, k_g, pos):
    B, S = h.shape[:2]
    z = h @ w_in
    q = z[..., :NSA_WIDTH].reshape(B, S, NSA_HEADS, HEAD_DIM)
    kv = z[..., NSA_WIDTH:NSA_WIDTH + 6 * KV_W].reshape(B, S, 6, NSA_KV_HEADS, HEAD_DIM)
    o = NSA_WIDTH + 6 * KV_W
    gates = jax.nn.sigmoid(z[..., o:o + GATE_COLS].astype(jnp.float32))
    gates = gates.reshape(B, S, NSA_KV_HEADS, NSA_GROUP, N_BRANCH).astype(h.dtype)
    glu = z[..., o + GATE_COLS:]
    u = glu[..., :CONV_CH] * jax.nn.sigmoid(glu[..., CONV_CH:])
    q = rope(rms_norm(q, q_g), pos).reshape(B, S, NSA_KV_HEADS, NSA_GROUP, HEAD_DIM)
    k_c = rope(rms_norm(kv[:, :, 0], k_g[0]), pos)
    k_s = rope(rms_norm(kv[:, :, 2], k_g[1]), pos)
    k_w = rope(rms_norm(kv[:, :, 4], k_g[2]), pos)
    return q, gates, k_c, kv[:, :, 1], k_s, kv[:, :, 3], k_w, kv[:, :, 5], u


def compress_rows(rows, pos_emb, w1, w2):
    B, L = rows.shape[:2]
    nc = (L - CMP_BLOCK) // CMP_STRIDE + 1
    idx = jnp.arange(nc)[:, None] * CMP_STRIDE + jnp.arange(CMP_BLOCK)[None, :]
    blk = rows[:, idx] + pos_emb[None, None, :, None, :]
    blk = jnp.swapaxes(blk, 2, 3).reshape(B, nc, NSA_KV_HEADS, CMP_BLOCK * HEAD_DIM)
    return jax.nn.gelu(blk @ w1) @ w2


def nsa_attend(q, qpos, gates, k_cmp, v_cmp, k_rows, v_rows, k_win, v_win, kwin_pos):
    B, QB = q.shape[:2]
    L = k_rows.shape[1]
    NC = k_cmp.shape[1]
    scale = HEAD_DIM ** -0.5
    s_c = jnp.einsum("bqkgd,bckd->bqkgc", q, k_cmp) * scale
    cmp_end = jnp.arange(NC) * CMP_STRIDE + (CMP_BLOCK - 1)
    p_c = masked_softmax(s_c, (cmp_end[None, :] <= qpos[:, None])[None, :, None, None, :])
    o_c = jnp.einsum("bqkgc,bckd->bqkgd", p_c.astype(v_cmp.dtype), v_cmp)
    nb = -(-L // SEL_BLOCK)
    c0 = jnp.arange(NC) * CMP_STRIDE
    s0 = jnp.arange(nb) * SEL_BLOCK
    overlap = jnp.clip(jnp.minimum(c0[:, None] + CMP_BLOCK, s0[None, :] + SEL_BLOCK)
                       - jnp.maximum(c0[:, None], s0[None, :]), 0, CMP_BLOCK).astype(jnp.float32) / CMP_BLOCK
    imp = jnp.einsum("bqkgc,cn->bqkn", p_c, overlap)
    blk = jnp.arange(nb)[None, :]
    cur = (qpos // SEL_BLOCK)[:, None]
    forced = (blk == 0) | (blk == cur) | (blk == cur - 1)
    visible = blk <= cur
    imp = jnp.where(forced[None, :, None, :], jnp.inf,
                    jnp.where(visible[None, :, None, :], imp, -jnp.inf))
    n_sel = min(N_SELECT, nb)
    _, sel = lax.top_k(imp, n_sel)
    tok = (sel[..., None] * SEL_BLOCK + jnp.arange(SEL_BLOCK)).reshape(B, QB, NSA_KV_HEADS, n_sel * SEL_BLOCK)
    tok_c = jnp.minimum(tok, L - 1)
    b_ix = jnp.arange(B)[:, None, None, None]
    h_ix = jnp.arange(NSA_KV_HEADS)[None, None, :, None]
    k_sel = k_rows[b_ix, tok_c, h_ix]
    v_sel = v_rows[b_ix, tok_c, h_ix]
    s_s = jnp.einsum("bqkgd,bqktd->bqkgt", q, k_sel) * scale
    p_s = masked_softmax(s_s, (tok <= qpos[None, :, None, None])[:, :, :, None, :])
    o_s = jnp.einsum("bqkgt,bqktd->bqkgd", p_s.astype(v_sel.dtype), v_sel)
    s_w = jnp.einsum("bqkgd,blkd->bqkgl", q, k_win) * scale
    dp = qpos[:, None] - kwin_pos[None, :]
    mask_w = (dp >= 0) & (dp < WINDOW) & (kwin_pos[None, :] >= 0)
    p_w = masked_softmax(s_w, mask_w[None, :, None, None, :])
    o_w = jnp.einsum("bqkgl,blkd->bqkgd", p_w.astype(v_win.dtype), v_win)
    o = gates[..., 0:1] * o_c + gates[..., 1:2] * o_s + gates[..., 2:3] * o_w
    return o.reshape(B, QB, NSA_WIDTH)


def nsa_prompt(q, gates, k_cmp, v_cmp, k_s, v_s, k_w, v_w):
    B, S = q.shape[:2]
    pad = jnp.zeros((B, WINDOW, NSA_KV_HEADS, HEAD_DIM), k_w.dtype)
    kw_pad = jnp.concatenate([pad, k_w], 1)
    vw_pad = jnp.concatenate([pad.astype(v_w.dtype), v_w], 1)

    def one_block(i):
        qs = i * Q_BLOCK
        qb = lax.dynamic_slice_in_dim(q, qs, Q_BLOCK, 1)
        gb = lax.dynamic_slice_in_dim(gates, qs, Q_BLOCK, 1)
        kwb = lax.dynamic_slice_in_dim(kw_pad, qs, WINDOW + Q_BLOCK, 1)
        vwb = lax.dynamic_slice_in_dim(vw_pad, qs, WINDOW + Q_BLOCK, 1)
        qpos = qs + jnp.arange(Q_BLOCK)
        kwpos = qs - WINDOW + jnp.arange(WINDOW + Q_BLOCK)
        return nsa_attend(qb, qpos, gb, k_cmp, v_cmp, k_s, v_s, kwb, vwb, kwpos)

    o = lax.map(one_block, jnp.arange(S // Q_BLOCK))
    return jnp.swapaxes(o, 0, 1).reshape(B, S, NSA_WIDTH)


def gather_pages(pool, page_table):
    g = pool[page_table]
    return g.reshape(page_table.shape[0], page_table.shape[1] * pool.shape[1], pool.shape[2], pool.shape[3])


def conformer_conv(u, buf, w, b, ln_g, ln_b):
    xx = jnp.concatenate([buf, u], 1)
    y = dwconv_valid(xx, w) + b
    y = jax.nn.silu(layer_norm(y, ln_g, ln_b))
    return y, xx[:, -(CONV_WIDTH - 1):]


def conv_ffn(h, buf, w_up, cw, cb, w_down):
    z = h @ w_up
    gate, up = z[..., :D_FF], z[..., D_FF:]
    gg = jnp.concatenate([buf, gate], 1)
    gc = dwconv_valid(gg, cw) + cb
    return (jax.nn.silu(gc) * up) @ w_down, gg[:, -(FFN_CONV_WIDTH - 1):]


def merge_and_ffn(x, o_nsa, u, conv_buf, ffn_buf, conv_w, conv_b, ln_g, ln_b, w_out, ffn_g, w_up, fcw, fcb, w_down):
    c, conv_state = conformer_conv(u, conv_buf, conv_w, conv_b, ln_g, ln_b)
    x = x + jnp.concatenate([o_nsa, c], -1) @ w_out
    f, ffn_state = conv_ffn(rms_norm(x, ffn_g), ffn_buf, w_up, fcw, fcb, w_down)
    return x + f, conv_state, ffn_state


def setup_inputs(seed: int = 0) -> dict:
    key = jax.random.key(seed)
    ks = jax.random.split(key, 40)
    n_pages = PAST_LEN // PAGE_SIZE
    n_pool = (DEC_BATCH * n_pages * 5) // 4
    win_buf = min(WINDOW, PAST_LEN)
    f32 = jnp.float32

    def nrm(k, shape, scale=1.0):
        return jax.random.normal(k, shape, f32) * scale

    pool_shape = (DEPTH, n_pool, PAGE_SIZE, NSA_KV_HEADS, HEAD_DIM)
    page_table = jax.random.permutation(ks[10], n_pool)[:DEC_BATCH * n_pages].reshape(DEC_BATCH, n_pages).astype(jnp.int32)
    return {
        "x_prompt": nrm(ks[0], (BATCH, SEQ, D_MODEL)),
        "x_sample": nrm(ks[1], (DEC_BATCH, DEC_SEQ, D_MODEL)),
        "cache_k_cmp": nrm(ks[2], pool_shape),
        "cache_v_cmp": nrm(ks[3], pool_shape),
        "cache_k_slc": nrm(ks[4], pool_shape),
        "cache_v_slc": nrm(ks[5], pool_shape),
        "state_k_win": nrm(ks[6], (DEPTH, DEC_BATCH, win_buf, NSA_KV_HEADS, HEAD_DIM)),
        "state_v_win": nrm(ks[7], (DEPTH, DEC_BATCH, win_buf, NSA_KV_HEADS, HEAD_DIM)),
        "state_conv": nrm(ks[8], (DEPTH, DEC_BATCH, CONV_WIDTH - 1, CONV_CH), 0.5),
        "state_ffn_conv": nrm(ks[9], (DEPTH, DEC_BATCH, FFN_CONV_WIDTH - 1, D_FF)),
        "page_table": page_table,
        "attn_norm": 1.0 + nrm(ks[11], (DEPTH, D_MODEL), 0.05),
        "w_in": nrm(ks[12], (DEPTH, D_MODEL, IN_COLS), D_MODEL ** -0.5),
        "q_norm": 1.0 + nrm(ks[13], (DEPTH, HEAD_DIM), 0.05),
        "k_norm": 1.0 + nrm(ks[14], (DEPTH, N_BRANCH, HEAD_DIM), 0.05),
        "kc_norm": 1.0 + nrm(ks[15], (DEPTH, HEAD_DIM), 0.05),
        "cmp_pos": nrm(ks[16], (DEPTH, CMP_BLOCK, HEAD_DIM), 0.1),
        "cmp_w1_k": nrm(ks[17], (DEPTH, CMP_BLOCK * HEAD_DIM, CMP_HIDDEN), (CMP_BLOCK * HEAD_DIM) ** -0.5),
        "cmp_w2_k": nrm(ks[18], (DEPTH, CMP_HIDDEN, HEAD_DIM), CMP_HIDDEN ** -0.5),
        "cmp_w1_v": nrm(ks[19], (DEPTH, CMP_BLOCK * HEAD_DIM, CMP_HIDDEN), (CMP_BLOCK * HEAD_DIM) ** -0.5),
        "cmp_w2_v": nrm(ks[20], (DEPTH, CMP_HIDDEN, HEAD_DIM), CMP_HIDDEN ** -0.5),
        "conv_w": nrm(ks[21], (DEPTH, CONV_WIDTH, CONV_CH), CONV_WIDTH ** -0.5),
        "conv_b": nrm(ks[22], (DEPTH, CONV_CH), 0.02),
        "conv_ln_g": 1.0 + nrm(ks[23], (DEPTH, CONV_CH), 0.05),
        "conv_ln_b": nrm(ks[24], (DEPTH, CONV_CH), 0.02),
        "w_out": nrm(ks[25], (DEPTH, MIX_WIDTH, D_MODEL), MIX_WIDTH ** -0.5),
        "ffn_norm": 1.0 + nrm(ks[26], (DEPTH, D_MODEL), 0.05),
        "w_up": nrm(ks[27], (DEPTH, D_MODEL, 2 * D_FF), D_MODEL ** -0.5),
        "ffn_conv_w": nrm(ks[28], (DEPTH, FFN_CONV_WIDTH, D_FF), FFN_CONV_WIDTH ** -0.5),
        "ffn_conv_b": nrm(ks[29], (DEPTH, D_FF), 0.02),
        "w_down": nrm(ks[30], (DEPTH, D_FF, D_MODEL), D_FF ** -0.5),
    }


def reference(x_prompt, x_sample, cache_k_cmp, cache_v_cmp, cache_k_slc, cache_v_slc,
              state_k_win, state_v_win, state_conv, state_ffn_conv, page_table,
              attn_norm, w_in, q_norm, k_norm, kc_norm, cmp_pos, cmp_w1_k, cmp_w2_k, cmp_w1_v, cmp_w2_v,
              conv_w, conv_b, conv_ln_g, conv_ln_b, w_out, ffn_norm, w_up, ffn_conv_w, ffn_conv_b, w_down):
    B, S = x_prompt.shape[:2]
    DB, DS = x_sample.shape[:2]
    past = page_table.shape[1] * cache_k_cmp.shape[2]
    win_buf = state_k_win.shape[2]
    pos_p = jnp.arange(S)
    pos_s = past + jnp.arange(DS)
    kwpos_s = past - win_buf + jnp.arange(win_buf + DS)
    p_win = min(WINDOW, S)
    y_p, y_s = x_prompt, x_sample
    pk_c, pv_c, pk_s, pv_s, pk_w, pv_w, p_cv, p_ff = [], [], [], [], [], [], [], []
    sk_c, sv_c, sk_s, sv_s, sk_w, sv_w, s_cv, s_ff = [], [], [], [], [], [], [], []
    for l in range(DEPTH):
        h = rms_norm(y_p, attn_norm[l])
        q, g, kc, vc, ksl, vsl, kw, vw, u = in_projection(h, w_in[l], q_norm[l], k_norm[l], pos_p)
        kcmp = rms_norm(compress_rows(kc, cmp_pos[l], cmp_w1_k[l], cmp_w2_k[l]), kc_norm[l])
        vcmp = compress_rows(vc, cmp_pos[l], cmp_w1_v[l], cmp_w2_v[l])
        o = nsa_prompt(q, g, kcmp, vcmp, ksl, vsl, kw, vw)
        conv0 = jnp.zeros((B, CONV_WIDTH - 1, CONV_CH), u.dtype)
        ffn0 = jnp.zeros((B, FFN_CONV_WIDTH - 1, D_FF), u.dtype)
        y_p, cv_state, ff_state = merge_and_ffn(y_p, o, u, conv0, ffn0, conv_w[l], conv_b[l], conv_ln_g[l], conv_ln_b[l],
                                                w_out[l], ffn_norm[l], w_up[l], ffn_conv_w[l], ffn_conv_b[l], w_down[l])
        pk_c.append(kc); pv_c.append(vc); pk_s.append(ksl); pv_s.append(vsl)
        pk_w.append(kw[:, -p_win:]); pv_w.append(vw[:, -p_win:]); p_cv.append(cv_state); p_ff.append(ff_state)

        h = rms_norm(y_s, attn_norm[l])
        q, g, kc, vc, ksl, vsl, kw, vw, u = in_projection(h, w_in[l], q_norm[l], k_norm[l], pos_s)
        kc_full = jnp.concatenate([gather_pages(cache_k_cmp[l], page_table), kc], 1)
        vc_full = jnp.concatenate([gather_pages(cache_v_cmp[l], page_table), vc], 1)
        ks_full = jnp.concatenate([gather_pages(cache_k_slc[l], page_table), ksl], 1)
        vs_full = jnp.concatenate([gather_pages(cache_v_slc[l], page_table), vsl], 1)
        kcmp = rms_norm(compress_rows(kc_full, cmp_pos[l], cmp_w1_k[l], cmp_w2_k[l]), kc_norm[l])
        vcmp = compress_rows(vc_full, cmp_pos[l], cmp_w1_v[l], cmp_w2_v[l])
        kw_full = jnp.concatenate([state_k_win[l], kw], 1)
        vw_full = jnp.concatenate([state_v_win[l], vw], 1)
        o = nsa_attend(q, pos_s, g, kcmp, vcmp, ks_full, vs_full, kw_full, vw_full, kwpos_s)
        y_s, cv_state, ff_state = merge_and_ffn(y_s, o, u, state_conv[l], state_ffn_conv[l], conv_w[l], conv_b[l],
                                                conv_ln_g[l], conv_ln_b[l], w_out[l], ffn_norm[l], w_up[l],
                                                ffn_conv_w[l], ffn_conv_b[l], w_down[l])
        sk_c.append(kc); sv_c.append(vc); sk_s.append(ksl); sv_s.append(vsl)
        sk_w.append(kw_full[:, -win_buf:]); sv_w.append(vw_full[:, -win_buf:]); s_cv.append(cv_state); s_ff.append(ff_state)
    st = jnp.stack
    return (y_p, y_s,
            st(pk_c), st(pv_c), st(pk_s), st(pv_s), st(pk_w), st(pv_w), st(p_cv), st(p_ff),
            st(sk_c), st(sv_c), st(sk_s), st(sv_s), st(sk_w), st(sv_w), st(s_cv), st(s_ff))
```

```python
import functools

import jax
import jax.numpy as jnp
from jax import lax
from jax.experimental import pallas as pl
from jax.experimental.pallas import tpu as pltpu

F32 = jnp.float32
BF16 = jnp.bfloat16

D_MODEL = 1024
HEAD_DIM = 64
NSA_HEADS = 8
NSA_KV_HEADS = 2
NSA_GROUP = NSA_HEADS // NSA_KV_HEADS
NSA_WIDTH = NSA_HEADS * HEAD_DIM
KV_W = NSA_KV_HEADS * HEAD_DIM
CONV_CH = D_MODEL - NSA_WIDTH
N_BRANCH = 3
CMP_BLOCK = 32
CMP_STRIDE = 16
CMP_HIDDEN = 128
SEL_BLOCK = 64
N_SELECT = 16
WINDOW = 512
CONV_WIDTH = 31
D_FF = 2816
FFN_CONV_WIDTH = 3
ROPE_THETA = 10000.0
EPS = 1e-6
Q_BLOCK = 128
GATE_COLS = NSA_HEADS * N_BRANCH
PAGE_SIZE = 128
SCALE = HEAD_DIM ** -0.5
NEG = -1e30

LANES = 128
SUBLANES = 8
VMEM_LIMIT = 52 * 1024 * 1024
CHUNK_W = CMP_STRIDE * KV_W
PAGE_CHUNKS = PAGE_SIZE // CMP_STRIDE
Z_COLS = NSA_WIDTH + 6 * KV_W + LANES + 2 * CONV_CH


def _round_up(x, m):
    return (x + m - 1) // m * m


def _bdot(a, b):
    return jnp.dot(a.astype(BF16), b.astype(BF16), preferred_element_type=F32)


def _bdot_nt(a, b):
    return lax.dot_general(a.astype(BF16), b.astype(BF16), (((1,), (1,)), ((), ())),
                           preferred_element_type=F32)


def _split_dot(a, b_bf16):
    hi = a.astype(BF16)
    lo = (a - hi.astype(F32)).astype(BF16)
    return (jnp.dot(hi, b_bf16, preferred_element_type=F32)
            + jnp.dot(lo, b_bf16, preferred_element_type=F32))


def _head_rms(x, bd, gain):
    ms = _split_dot(x * x, bd) * (1.0 / HEAD_DIM)
    return x * lax.rsqrt(ms + EPS) * gain


def _rope(y, cos, sin_signed):
    w = y.shape[-1]
    half = HEAD_DIM // 2
    fwd = pltpu.roll(y, w - half, 1)
    bwd = pltpu.roll(y, half, 1)
    lane = lax.broadcasted_iota(jnp.int32, y.shape, 1)
    swapped = jnp.where((lane & (HEAD_DIM - 1)) < half, fwd, bwd)
    return y * cos + swapped * sin_signed


def _masked_softmax(s, mask):
    s = jnp.where(mask, s, NEG)
    m = jnp.max(s, -1, keepdims=True)
    e = jnp.where(mask, jnp.exp(s - m), 0.0)
    return e / jnp.maximum(jnp.sum(e, -1, keepdims=True), 1e-30)


def _const_spec(shape):
    nd = len(shape)
    return pl.BlockSpec(shape, lambda *_: (0,) * nd, pipeline_mode=pl.Buffered(1))


def _inproj_kernel(x_ref, g_ref, w_ref, qg_ref, kg_ref, cos_ref, sin_ref, bd_ref,
                   q_ref, kc_ref, vc_ref, ks_ref, vs_ref, kw_ref, vw_ref, gt_ref, u_ref):
    x = x_ref[...]
    h = x * lax.rsqrt(jnp.mean(x * x, -1, keepdims=True) + EPS) * g_ref[...]
    z = jnp.dot(h.astype(BF16), w_ref[...], preferred_element_type=F32)
    cos = cos_ref[...]
    sin = sin_ref[...]
    bdq = bd_ref[...]
    bdk = bdq[:KV_W, :KV_W]
    nrep = NSA_WIDTH // KV_W
    q = _head_rms(z[:, :NSA_WIDTH], bdq, qg_ref[...])
    q_ref[...] = _rope(q, jnp.concatenate([cos] * nrep, axis=1), jnp.concatenate([sin] * nrep, axis=1))
    kg = kg_ref[...]
    o = NSA_WIDTH
    kc_ref[...] = _rope(_head_rms(z[:, o:o + KV_W], bdk, kg[:, 0:KV_W]), cos, sin)
    vc_ref[...] = z[:, o + KV_W:o + 2 * KV_W]
    ks_ref[...] = _rope(_head_rms(z[:, o + 2 * KV_W:o + 3 * KV_W], bdk, kg[:, KV_W:2 * KV_W]), cos, sin)
    vs_ref[...] = z[:, o + 3 * KV_W:o + 4 * KV_W]
    kw_ref[...] = _rope(_head_rms(z[:, o + 4 * KV_W:o + 5 * KV_W], bdk, kg[:, 2 * KV_W:3 * KV_W]), cos, sin)
    vw_ref[...] = z[:, o + 5 * KV_W:o + 6 * KV_W]
    o += 6 * KV_W
    gt_ref[...] = jax.nn.sigmoid(z[:, o:o + LANES])
    o += LANES
    u_ref[...] = z[:, o:o + CONV_CH] * jax.nn.sigmoid(z[:, o + CONV_CH:o + 2 * CONV_CH])


def _inproj(x, g, w, qg, kg, cos, sin, bd, tm):
    t = x.shape[0]
    nt = t // tm
    npos = cos.shape[0] // tm
    row = lambda i: (i, 0)
    pos = lambda i: (i % npos, 0)
    widths = [NSA_WIDTH] + [KV_W] * 6 + [LANES, CONV_CH]
    return pl.pallas_call(
        _inproj_kernel,
        grid=(nt,),
        in_specs=[pl.BlockSpec((tm, D_MODEL), row), _const_spec((1, D_MODEL)), _const_spec((D_MODEL, Z_COLS)),
                  _const_spec((1, NSA_WIDTH)), _const_spec((1, 3 * KV_W)),
                  pl.BlockSpec((tm, KV_W), pos), pl.BlockSpec((tm, KV_W), pos),
                  _const_spec((NSA_WIDTH, NSA_WIDTH))],
        out_specs=[pl.BlockSpec((tm, wd), row) for wd in widths],
        out_shape=[jax.ShapeDtypeStruct((t, wd), F32) for wd in widths],
        compiler_params=pltpu.CompilerParams(dimension_semantics=("arbitrary",), vmem_limit_bytes=VMEM_LIMIT),
        name="inproj",
    )(x, g, w, qg, kg, cos, sin, bd)


def _page_copies(pt_ref, pool_ref, buf, sem, batch, slot, n_pages, rows_per_page):
    return [pltpu.make_async_copy(pool_ref.at[pt_ref[batch * n_pages + p]],
                                  buf.at[slot, pl.ds(p * rows_per_page, rows_per_page)],
                                  sem.at[slot])
            for p in range(n_pages)]


def _compress_kernel(pt_ref, pool_ref, wab_ref, pos_ref, w2_ref, g_ref, bd_ref, out_ref, buf, sem,
                     *, n_pages, norm):
    b = pl.program_id(0)
    nb = pl.num_programs(0)
    copies = functools.partial(_page_copies, pt_ref, pool_ref, buf, sem,
                               n_pages=n_pages, rows_per_page=PAGE_CHUNKS)

    @pl.when(b == 0)
    def _():
        for c in copies(0, 0):
            c.start()

    @pl.when(b + 1 < nb)
    def _():
        for c in copies(b + 1, (b + 1) % 2):
            c.start()

    slot = b % 2
    for c in copies(b, slot):
        c.wait()

    x = buf[slot]
    nch = x.shape[0]
    hid2 = 2 * CMP_HIDDEN
    wab = wab_ref[...]
    z = jnp.dot(x.astype(BF16), wab, preferred_element_type=F32)
    pos = pos_ref[...].astype(BF16)
    bias = (jnp.dot(pos[0:SUBLANES], wab[:, :hid2], preferred_element_type=F32)
            + jnp.dot(pos[SUBLANES:], wab[:, hid2:], preferred_element_type=F32))[0:1]
    nxt = pltpu.roll(z[:, hid2:], nch - 1, 0)
    hid = jax.nn.gelu(z[:, :hid2] + nxt + bias)
    y = jnp.dot(hid.astype(BF16), w2_ref[...], preferred_element_type=F32)
    if norm:
        y = _head_rms(y, bd_ref[...], g_ref[...])
    rowi = lax.broadcasted_iota(jnp.int32, y.shape, 0)
    out_ref[0] = jnp.where(rowi < nch - 1, y, 0.0)


def _compress(page_table, pool, wab, pos, w2, g, bd, n_batch, norm):
    n_pages = page_table.shape[0] // n_batch
    nch = n_pages * PAGE_CHUNKS
    pool3 = pool.reshape(-1, PAGE_CHUNKS, CHUNK_W)
    grid_spec = pltpu.PrefetchScalarGridSpec(
        num_scalar_prefetch=1,
        grid=(n_batch,),
        in_specs=[pl.BlockSpec(memory_space=pl.ANY),
                  pl.BlockSpec(wab.shape, lambda b, pt: (0, 0)),
                  pl.BlockSpec(pos.shape, lambda b, pt: (0, 0)),
                  pl.BlockSpec(w2.shape, lambda b, pt: (0, 0)),
                  pl.BlockSpec(g.shape, lambda b, pt: (0, 0)),
                  pl.BlockSpec(bd.shape, lambda b, pt: (0, 0))],
        out_specs=pl.BlockSpec((1, nch, KV_W), lambda b, pt: (b, 0, 0)),
        scratch_shapes=[pltpu.VMEM((2, nch, CHUNK_W), F32), pltpu.SemaphoreType.DMA((2,))],
    )
    return pl.pallas_call(
        functools.partial(_compress_kernel, n_pages=n_pages, norm=norm),
        grid_spec=grid_spec,
        out_shape=jax.ShapeDtypeStruct((n_batch, nch, KV_W), F32),
        compiler_params=pltpu.CompilerParams(dimension_semantics=("arbitrary",), vmem_limit_bytes=VMEM_LIMIT),
        name="compress",
    )(page_table, pool3, wab, pos, w2, g, bd)


def _select_mask(imp, cur, n_blocks):
    n = lax.broadcasted_iota(jnp.int32, (1, imp.shape[1]), 1)
    forced = (n == 0) | (n == cur) | (n == cur - 1)
    v = jnp.where(forced, jnp.inf, jnp.where(n <= cur, imp, -jnp.inf))
    rank = jnp.zeros(imp.shape, jnp.int32)
    for m in range(n_blocks):
        vm = v[:, m:m + 1]
        before = (n > m).astype(jnp.int32)
        rank = rank + jnp.where(vm > v, 1, jnp.where(vm == v, before, 0))
    return jnp.where(rank < min(N_SELECT, n_blocks), 1.0, 0.0).astype(BF16)


KEY_BLOCK = 512


def _attn_prompt_kernel(q_ref, gt_ref, kcmp_ref, vcmp_ref, ks_ref, vs_ref, kw_ref, vw_ref, ov_ref, ex_ref,
                        o_ref, *, seq):
    i = pl.program_id(1)
    qs = i * Q_BLOCK
    q = q_ref[...] * SCALE
    gt = gt_ref[...]
    rows = NSA_GROUP * Q_BLOCK
    rowi = lax.broadcasted_iota(jnp.int32, (rows, 1), 0)
    qpos4 = qs + (rowi & (Q_BLOCK - 1))
    qpos1 = qs + lax.broadcasted_iota(jnp.int32, (Q_BLOCK, 1), 0)
    n_cmp = kcmp_ref.shape[1]
    n_blocks = seq // SEL_BLOCK
    win_keys = WINDOW + Q_BLOCK
    pieces = []
    for h in range(NSA_KV_HEADS):
        hs = slice(h * HEAD_DIM, (h + 1) * HEAD_DIM)
        q4 = jnp.concatenate([q[:, (h * NSA_GROUP + g) * HEAD_DIM:(h * NSA_GROUP + g + 1) * HEAD_DIM]
                              for g in range(NSA_GROUP)], axis=0).astype(BF16)
        s_c = _bdot_nt(q4, kcmp_ref[0][:, hs])
        c = lax.broadcasted_iota(jnp.int32, (1, n_cmp), 1)
        p_c = _masked_softmax(s_c, (c * CMP_STRIDE + (CMP_BLOCK - 1)) <= qpos4)
        o_c = _bdot(p_c, vcmp_ref[0][:, hs])
        p_sum = p_c[0:Q_BLOCK]
        for g in range(1, NSA_GROUP):
            p_sum = p_sum + p_c[g * Q_BLOCK:(g + 1) * Q_BLOCK]
        imp = _split_dot(p_sum, ov_ref[...])
        sel = _select_mask(imp, qpos1 // SEL_BLOCK, n_blocks)

        def sel_body(j, carry):
            m_i, l_i, acc = carry
            k0 = pl.multiple_of(j * KEY_BLOCK, KEY_BLOCK)
            kb = ks_ref[0, pl.ds(k0, KEY_BLOCK), :][:, hs]
            vb = vs_ref[0, pl.ds(k0, KEY_BLOCK), :][:, hs]
            s = _bdot_nt(q4, kb)
            selm = jnp.dot(sel, ex_ref[j], preferred_element_type=F32)
            selm = jnp.concatenate([selm] * NSA_GROUP, axis=0)
            tok = k0 + lax.broadcasted_iota(jnp.int32, (1, KEY_BLOCK), 1)
            mask = (selm > 0.5) & (tok <= qpos4)
            s = jnp.where(mask, s, NEG)
            m_new = jnp.maximum(m_i, jnp.max(s, -1, keepdims=True))
            alpha = jnp.exp(m_i - m_new)
            e = jnp.where(mask, jnp.exp(s - m_new), 0.0)
            l_new = alpha * l_i + jnp.sum(e, -1, keepdims=True)
            acc_new = alpha * acc + _bdot(e, vb)
            return m_new, l_new, acc_new

        n_kb = (qs + Q_BLOCK + KEY_BLOCK - 1) // KEY_BLOCK
        m0 = jnp.full((rows, 1), NEG, F32)
        l0 = jnp.zeros((rows, 1), F32)
        a0 = jnp.zeros((rows, HEAD_DIM), F32)
        _, l_s, acc_s = lax.fori_loop(0, n_kb, sel_body, (m0, l0, a0))
        o_s = acc_s / jnp.maximum(l_s, 1e-30)
        w0 = pl.multiple_of(jnp.maximum(qs - WINDOW, 0), Q_BLOCK)
        kwb = kw_ref[0, pl.ds(w0, win_keys), :][:, hs]
        vwb = vw_ref[0, pl.ds(w0, win_keys), :][:, hs]
        s_w = _bdot_nt(q4, kwb)
        dp = qpos4 - (w0 + lax.broadcasted_iota(jnp.int32, (1, win_keys), 1))
        p_w = _masked_softmax(s_w, (dp >= 0) & (dp < WINDOW))
        o_w = _bdot(p_w, vwb)
        for g in range(NSA_GROUP):
            j = (h * NSA_GROUP + g) * N_BRANCH
            r = slice(g * Q_BLOCK, (g + 1) * Q_BLOCK)
            pieces.append(gt[:, j:j + 1] * o_c[r] + gt[:, j + 1:j + 2] * o_s[r] + gt[:, j + 2:j + 3] * o_w[r])
    o_ref[...] = jnp.concatenate(pieces, axis=1)


def _attn_prompt(q, gt, kcmp, vcmp, ks, vs, kw, vw, ov, ex, n_batch, seq):
    nq = seq // Q_BLOCK
    n_cmp = kcmp.shape[1]
    tok = lambda b, i: (b * nq + i, 0)
    per_b = lambda b, i: (b, 0, 0)
    kv3 = lambda a: a.reshape(n_batch, seq, KV_W)
    return pl.pallas_call(
        functools.partial(_attn_prompt_kernel, seq=seq),
        grid=(n_batch, nq),
        in_specs=[pl.BlockSpec((Q_BLOCK, NSA_WIDTH), tok), pl.BlockSpec((Q_BLOCK, LANES), tok),
                  pl.BlockSpec((1, n_cmp, KV_W), per_b), pl.BlockSpec((1, n_cmp, KV_W), per_b),
                  pl.BlockSpec((1, seq, KV_W), per_b), pl.BlockSpec((1, seq, KV_W), per_b),
                  pl.BlockSpec((1, seq, KV_W), per_b), pl.BlockSpec((1, seq, KV_W), per_b),
                  _const_spec(ov.shape), _const_spec(ex.shape)],
        out_specs=pl.BlockSpec((Q_BLOCK, NSA_WIDTH), tok),
        out_shape=jax.ShapeDtypeStruct((n_batch * seq, NSA_WIDTH), F32),
        compiler_params=pltpu.CompilerParams(dimension_semantics=("arbitrary", "arbitrary"),
                                             vmem_limit_bytes=VMEM_LIMIT),
        name="attn_prompt",
    )(q, gt, kcmp, vcmp, kv3(ks), kv3(vs), kv3(kw), kv3(vw), ov, ex)


def _attn_sample_kernel(pt_ref, q_ref, gt_ref, kcmp_ref, vcmp_ref, ksn_ref, vsn_ref, kws_ref, vws_ref,
                        kwn_ref, vwn_ref, ov_ref, ex_ref, kpool_ref, vpool_ref,
                        o_ref, kbuf, vbuf, kwbuf, vwbuf, ksem, vsem, *, n_pages, past):
    b = pl.program_id(0)
    nb = pl.num_programs(0)
    kcopies = functools.partial(_page_copies, pt_ref, kpool_ref, kbuf, ksem,
                                n_pages=n_pages, rows_per_page=PAGE_SIZE)
    vcopies = functools.partial(_page_copies, pt_ref, vpool_ref, vbuf, vsem,
                                n_pages=n_pages, rows_per_page=PAGE_SIZE)

    @pl.when(b == 0)
    def _():
        for c in kcopies(0, 0) + vcopies(0, 0):
            c.start()

    @pl.when(b + 1 < nb)
    def _():
        for c in kcopies(b + 1, (b + 1) % 2) + vcopies(b + 1, (b + 1) % 2):
            c.start()

    slot = b % 2
    ds = q_ref.shape[1]
    keys = kbuf.shape[1]
    kbuf[slot, pl.ds(past, ds), :] = ksn_ref[0]
    vbuf[slot, pl.ds(past, ds), :] = vsn_ref[0]
    pad = jnp.zeros((keys - past - ds, KV_W), F32)
    kbuf[slot, pl.ds(past + ds, keys - past - ds), :] = pad
    vbuf[slot, pl.ds(past + ds, keys - past - ds), :] = pad
    nwin = kws_ref.shape[1]
    wkeys = kwbuf.shape[0]
    kwbuf[pl.ds(0, nwin), :] = kws_ref[0]
    vwbuf[pl.ds(0, nwin), :] = vws_ref[0]
    kwbuf[pl.ds(nwin, ds), :] = kwn_ref[0]
    vwbuf[pl.ds(nwin, ds), :] = vwn_ref[0]
    wpad = jnp.zeros((wkeys - nwin - ds, KV_W), F32)
    kwbuf[pl.ds(nwin + ds, wkeys - nwin - ds), :] = wpad
    vwbuf[pl.ds(nwin + ds, wkeys - nwin - ds), :] = wpad

    q = q_ref[0] * SCALE
    gt = gt_ref[0]
    zero = jnp.zeros((ds, HEAD_DIM), F32)
    blocks = []
    for h in range(NSA_KV_HEADS):
        for g in range(NSA_GROUP):
            hd = h * NSA_GROUP + g
            piece = q[:, hd * HEAD_DIM:(hd + 1) * HEAD_DIM]
            blocks.append(jnp.concatenate([piece, zero] if h == 0 else [zero, piece], axis=1))
    qp = jnp.concatenate(blocks, axis=0).astype(BF16)
    rows = qp.shape[0]
    per_head = NSA_GROUP * ds
    qpos = past + (lax.broadcasted_iota(jnp.int32, (rows, 1), 0) & (ds - 1))

    n_cmp = kcmp_ref.shape[1]
    s_c = _bdot_nt(qp, kcmp_ref[0])
    c = lax.broadcasted_iota(jnp.int32, (1, n_cmp), 1)
    p_c = _masked_softmax(s_c, (c * CMP_STRIDE + (CMP_BLOCK - 1)) <= qpos)
    o_c = _bdot(p_c, vcmp_ref[0])
    sums = []
    for h in range(NSA_KV_HEADS):
        acc = p_c[h * per_head:h * per_head + ds]
        for g in range(1, NSA_GROUP):
            acc = acc + p_c[h * per_head + g * ds:h * per_head + (g + 1) * ds]
        sums.append(acc)
    imp = _split_dot(jnp.concatenate(sums, axis=0), ov_ref[...])
    qpos_sel = past + (lax.broadcasted_iota(jnp.int32, (NSA_KV_HEADS * ds, 1), 0) & (ds - 1))
    n_blocks = -(-(past + ds) // SEL_BLOCK)
    sel = _select_mask(imp, qpos_sel // SEL_BLOCK, n_blocks)
    sel = jnp.concatenate([sel[h * ds:(h + 1) * ds] for h in range(NSA_KV_HEADS) for _ in range(NSA_GROUP)],
                          axis=0)

    for c_ in kcopies(b, slot) + vcopies(b, slot):
        c_.wait()

    selm = jnp.dot(sel, ex_ref[...], preferred_element_type=F32)
    s_s = _bdot_nt(qp, kbuf[slot])
    tok = lax.broadcasted_iota(jnp.int32, (1, keys), 1)
    p_s = _masked_softmax(s_s, (selm > 0.5) & (tok <= qpos))
    o_s = _bdot(p_s, vbuf[slot])
    s_w = _bdot_nt(qp, kwbuf[...])
    dp = qpos - (past - nwin + lax.broadcasted_iota(jnp.int32, (1, wkeys), 1))
    p_w = _masked_softmax(s_w, (dp >= 0) & (dp < WINDOW))
    o_w = _bdot(p_w, vwbuf[...])

    pieces = []
    for h in range(NSA_KV_HEADS):
        hs = slice(h * HEAD_DIM, (h + 1) * HEAD_DIM)
        for g in range(NSA_GROUP):
            j = (h * NSA_GROUP + g) * N_BRANCH
            r = slice(h * per_head + g * ds, h * per_head + (g + 1) * ds)
            pieces.append(gt[:, j:j + 1] * o_c[r, hs] + gt[:, j + 1:j + 2] * o_s[r, hs]
                          + gt[:, j + 2:j + 3] * o_w[r, hs])
    o_ref[0] = jnp.concatenate(pieces, axis=1)


def _attn_sample(page_table, q, gt, kcmp, vcmp, ksn, vsn, kws, vws, kwn, vwn, ov, ex, kpool, vpool, past):
    n_batch, ds = q.shape[:2]
    n_pages = page_table.shape[0] // n_batch
    n_cmp = kcmp.shape[1]
    nwin = kws.shape[1]
    keys = ex.shape[1]
    wkeys = _round_up(nwin + ds, LANES)
    per_b = lambda b, pt: (b, 0, 0)
    full = lambda a: pl.BlockSpec(a.shape, lambda b, pt: (0,) * a.ndim)
    grid_spec = pltpu.PrefetchScalarGridSpec(
        num_scalar_prefetch=1,
        grid=(n_batch,),
        in_specs=[pl.BlockSpec((1, ds, NSA_WIDTH), per_b), pl.BlockSpec((1, ds, LANES), per_b),
                  pl.BlockSpec((1, n_cmp, KV_W), per_b), pl.BlockSpec((1, n_cmp, KV_W), per_b),
                  pl.BlockSpec((1, ds, KV_W), per_b), pl.BlockSpec((1, ds, KV_W), per_b),
                  pl.BlockSpec((1, nwin, KV_W), per_b), pl.BlockSpec((1, nwin, KV_W), per_b),
                  pl.BlockSpec((1, ds, KV_W), per_b), pl.BlockSpec((1, ds, KV_W), per_b),
                  full(ov), full(ex),
                  pl.BlockSpec(memory_space=pl.ANY), pl.BlockSpec(memory_space=pl.ANY)],
        out_specs=pl.BlockSpec((1, ds, NSA_WIDTH), per_b),
        scratch_shapes=[pltpu.VMEM((2, keys, KV_W), F32), pltpu.VMEM((2, keys, KV_W), F32),
                        pltpu.VMEM((wkeys, KV_W), F32), pltpu.VMEM((wkeys, KV_W), F32),
                        pltpu.SemaphoreType.DMA((2,)), pltpu.SemaphoreType.DMA((2,))],
    )
    return pl.pallas_call(
        functools.partial(_attn_sample_kernel, n_pages=n_pages, past=past),
        grid_spec=grid_spec,
        out_shape=jax.ShapeDtypeStruct((n_batch, ds, NSA_WIDTH), F32),
        compiler_params=pltpu.CompilerParams(dimension_semantics=("arbitrary",), vmem_limit_bytes=VMEM_LIMIT),
        name="attn_sample",
    )(page_table, q, gt, kcmp, vcmp, ksn, vsn, kws, vws, kwn, vwn, ov, ex, kpool, vpool)


CONV_ROWS = 64
FF_SPLIT = 2


def _merge_ffn_kernel(x_ref, o_ref, u_ref, cbuf_ref, fbuf_ref, cw_ref, cb_ref, lng_ref, lnb_ref, wout_ref,
                      fg_ref, wup_ref, fcw_ref, fcb_ref, wdn_ref,
                      y_ref, tail_ref, xx, cc, gg, *, rs):
    t = pl.program_id(1)
    tm = x_ref.shape[0]
    hpc = xx.shape[0] - tm
    hpf = gg.shape[0] - tm
    off_c = hpc - (CONV_WIDTH - 1) * rs
    off_f = hpf - (FFN_CONV_WIDTH - 1) * rs

    @pl.when(t == 0)
    def _():
        xx[pl.ds(0, hpc), :] = cbuf_ref[0]
        gg[pl.ds(0, hpf), :] = fbuf_ref[0]

    xx[pl.ds(hpc, tm), :] = u_ref[...]
    rc = min(CONV_ROWS, tm)
    for c0 in range(0, CONV_CH, LANES):
        for r0 in range(0, tm, rc):
            acc = jnp.zeros((rc, LANES), F32)
            for k in range(CONV_WIDTH):
                acc = acc + cw_ref[k:k + 1, c0:c0 + LANES] * xx[pl.ds(off_c + k * rs + r0, rc), c0:c0 + LANES]
            cc[pl.ds(r0, rc), c0:c0 + LANES] = acc
    y = cc[...] + cb_ref[...]
    mu = jnp.mean(y, -1, keepdims=True)
    var = jnp.mean(jnp.square(y - mu), -1, keepdims=True)
    y = (y - mu) * lax.rsqrt(var + EPS) * lng_ref[...] + lnb_ref[...]
    c = y * jax.nn.sigmoid(y)
    x1 = (x_ref[...] + _bdot(o_ref[...], wout_ref[0:NSA_WIDTH, :])
          + _bdot(c, wout_ref[NSA_WIDTH:NSA_WIDTH + CONV_CH, :]))
    h = (x1 * lax.rsqrt(jnp.mean(x1 * x1, -1, keepdims=True) + EPS) * fg_ref[...]).astype(BF16)
    fw = D_FF // FF_SPLIT
    f = jnp.zeros((tm, D_MODEL), F32)
    for s in range(FF_SPLIT):
        c0 = s * fw
        gg[pl.ds(hpf, tm), c0:c0 + fw] = jnp.dot(h, wup_ref[:, c0:c0 + fw], preferred_element_type=F32)
        up = jnp.dot(h, wup_ref[:, D_FF + c0:D_FF + c0 + fw], preferred_element_type=F32)
        gc = fcb_ref[:, c0:c0 + fw]
        for k in range(FFN_CONV_WIDTH):
            gc = gc + fcw_ref[k:k + 1, c0:c0 + fw] * gg[pl.ds(off_f + k * rs, tm), c0:c0 + fw]
        act = gc * jax.nn.sigmoid(gc) * up
        f = f + jnp.dot(act.astype(BF16), wdn_ref[c0:c0 + fw, :], preferred_element_type=F32)
    y_ref[...] = x1 + f
    tail_ref[0] = gg[pl.ds(tm, hpf), :]
    xx[pl.ds(0, hpc), :] = xx[pl.ds(tm, hpc), :]
    gg[pl.ds(0, hpf), :] = gg[pl.ds(tm, hpf), :]


def _merge_ffn(x, o, u, cbuf, fbuf, cw, cb, lng, lnb, wout, fg, wup, fcw, fcb, wdn, tm, rs):
    t = x.shape[0]
    n_batch = cbuf.shape[0]
    nt = t // n_batch // tm
    hpc = cbuf.shape[1]
    hpf = fbuf.shape[1]
    tok = lambda b, i: (b * nt + i, 0)
    per_b = lambda b, i: (b, 0, 0)
    return pl.pallas_call(
        functools.partial(_merge_ffn_kernel, rs=rs),
        grid=(n_batch, nt),
        in_specs=[pl.BlockSpec((tm, D_MODEL), tok), pl.BlockSpec((tm, NSA_WIDTH), tok),
                  pl.BlockSpec((tm, CONV_CH), tok),
                  pl.BlockSpec((1, hpc, CONV_CH), per_b), pl.BlockSpec((1, hpf, D_FF), per_b),
                  _const_spec(cw.shape), _const_spec(cb.shape), _const_spec(lng.shape), _const_spec(lnb.shape),
                  _const_spec(wout.shape), _const_spec(fg.shape), _const_spec(wup.shape),
                  _const_spec(fcw.shape), _const_spec(fcb.shape), _const_spec(wdn.shape)],
        out_specs=[pl.BlockSpec((tm, D_MODEL), tok), pl.BlockSpec((1, hpf, D_FF), per_b)],
        out_shape=[jax.ShapeDtypeStruct((t, D_MODEL), F32), jax.ShapeDtypeStruct((n_batch, hpf, D_FF), F32)],
        scratch_shapes=[pltpu.VMEM((hpc + tm, CONV_CH), F32), pltpu.VMEM((tm, CONV_CH), F32),
                        pltpu.VMEM((hpf + tm, D_FF), F32)],
        compiler_params=pltpu.CompilerParams(dimension_semantics=("arbitrary", "arbitrary"),
                                             vmem_limit_bytes=VMEM_LIMIT),
        name="merge_ffn",
    )(x, o, u, cbuf, fbuf, cw, cb, lng, lnb, wout, fg, wup, fcw, fcb, wdn)


def _rope_tables(pos):
    half = HEAD_DIM // 2
    inv = ROPE_THETA ** (-jnp.arange(half, dtype=F32) / half)
    ang = pos.astype(F32)[:, None] * inv[None, :]
    cos, sin = jnp.cos(ang), jnp.sin(ang)
    reps = KV_W // HEAD_DIM
    return (jnp.tile(jnp.concatenate([cos, cos], -1), (1, reps)),
            jnp.tile(jnp.concatenate([-sin, sin], -1), (1, reps)))


def _head_sum_matrix(width):
    i = jnp.arange(width) // HEAD_DIM
    return (i[:, None] == i[None, :]).astype(BF16)


def _overlap_matrix(n_cmp_rows, n_blocks, n_lanes):
    c0 = jnp.arange(n_cmp_rows) * CMP_STRIDE
    s0 = jnp.arange(n_lanes) * SEL_BLOCK
    ov = jnp.clip(jnp.minimum(c0[:, None] + CMP_BLOCK, s0[None, :] + SEL_BLOCK)
                  - jnp.maximum(c0[:, None], s0[None, :]), 0, CMP_BLOCK).astype(F32) / CMP_BLOCK
    ov = jnp.where(jnp.arange(n_lanes)[None, :] < n_blocks, ov, 0.0)
    return ov.astype(BF16)


def _expand_matrix(n_lanes, n_blocks, n_keys):
    n = jnp.arange(n_lanes)[:, None]
    t = jnp.arange(n_keys)[None, :] // SEL_BLOCK
    return ((n == t) & (n < n_blocks)).astype(BF16)


def _compress_weights(w1, w2, pos):
    half = CMP_STRIDE * HEAD_DIM
    eye = jnp.eye(NSA_KV_HEADS, dtype=F32)

    def spread(wpart):
        w3 = wpart.reshape(CMP_STRIDE, HEAD_DIM, CMP_HIDDEN)
        return jnp.einsum("tdh,kj->tkdjh", w3, eye).reshape(CHUNK_W, NSA_KV_HEADS * CMP_HIDDEN)

    wab = jnp.concatenate([spread(w1[:half]), spread(w1[half:])], axis=1).astype(BF16)
    w2bd = jnp.einsum("hd,kj->khjd", w2, eye).reshape(NSA_KV_HEADS * CMP_HIDDEN, KV_W).astype(BF16)

    def pos_row(p):
        row = jnp.tile(p[:, None, :], (1, NSA_KV_HEADS, 1)).reshape(1, CHUNK_W)
        return jnp.concatenate([row, jnp.zeros((SUBLANES - 1, CHUNK_W), F32)], axis=0)

    posab = jnp.concatenate([pos_row(pos[:CMP_STRIDE]), pos_row(pos[CMP_STRIDE:])], axis=0)
    return wab, w2bd, posab


def _tile_lanes(v, reps):
    return jnp.tile(v, reps)[None, :]


PROMPT_TM = 512
FFN_TM = 256


def kernel(x_prompt, x_sample, cache_k_cmp, cache_v_cmp, cache_k_slc, cache_v_slc, state_k_win, state_v_win,
           state_conv, state_ffn_conv, page_table, attn_norm, w_in, q_norm, k_norm, kc_norm, cmp_pos,
           cmp_w1_k, cmp_w2_k, cmp_w1_v, cmp_w2_v, conv_w, conv_b, conv_ln_g, conv_ln_b, w_out, ffn_norm,
           w_up, ffn_conv_w, ffn_conv_b, w_down):
    B, S = x_prompt.shape[:2]
    DB, DS = x_sample.shape[:2]
    depth = w_in.shape[0]
    n_pages = page_table.shape[1]
    past = n_pages * PAGE_SIZE
    win_buf = state_k_win.shape[2]
    p_win = min(WINDOW, S)
    TS = DB * DS

    cos_p, sin_p = _rope_tables(jnp.arange(S))
    cos_s, sin_s = _rope_tables(jnp.repeat(past + jnp.arange(DS), DB))
    bdq = _head_sum_matrix(NSA_WIDTH)
    bdk = _head_sum_matrix(KV_W)
    ncp = S // CMP_STRIDE
    ncs = past // CMP_STRIDE
    nb_p = S // SEL_BLOCK
    nb_s = -(-(past + DS) // SEL_BLOCK)
    nl_s = _round_up(nb_s, LANES)
    keys_s = _round_up(past + DS, LANES)
    ov_p = _overlap_matrix(ncp, nb_p, LANES)
    ex_p = _expand_matrix(LANES, nb_p, S).reshape(LANES, S // KEY_BLOCK, KEY_BLOCK).transpose(1, 0, 2)
    ov_s = _overlap_matrix(ncs, nb_s, nl_s)
    ex_s = _expand_matrix(nl_s, nb_s, keys_s)
    pt_flat = page_table.reshape(-1).astype(jnp.int32)
    pt_prompt = jnp.arange(B * (S // PAGE_SIZE), dtype=jnp.int32)

    o_g = NSA_WIDTH + 6 * KV_W
    w_cat = jnp.concatenate([w_in[:, :, :o_g],
                             jnp.pad(w_in[:, :, o_g:o_g + GATE_COLS], ((0, 0), (0, 0), (0, LANES - GATE_COLS))),
                             w_in[:, :, o_g + GATE_COLS:]], axis=2).astype(BF16)
    w_out_b = w_out.astype(BF16)
    w_up_b = w_up.astype(BF16)
    w_down_b = w_down.astype(BF16)
    conv_w_p = jnp.pad(conv_w, ((0, 0), (0, _round_up(CONV_WIDTH, SUBLANES) - CONV_WIDTH), (0, 0)))
    fcw_p = jnp.pad(ffn_conv_w, ((0, 0), (0, SUBLANES - FFN_CONV_WIDTH), (0, 0)))

    hpc_p = _round_up(CONV_WIDTH - 1, SUBLANES)
    hpf_p = _round_up(FFN_CONV_WIDTH - 1, SUBLANES)
    conv0 = jnp.zeros((B, hpc_p, CONV_CH), F32)
    ffn0 = jnp.zeros((B, hpf_p, D_FF), F32)

    y_p = x_prompt.reshape(B * S, D_MODEL)
    y_s = x_sample.transpose(1, 0, 2).reshape(TS, D_MODEL)

    def to_bm(a):
        return a.reshape(DS, DB, a.shape[-1]).transpose(1, 0, 2)

    def heads(a, lead):
        return a.reshape(lead + (NSA_KV_HEADS, HEAD_DIM))

    outs = [[] for _ in range(16)]
    for l in range(depth):
        qg = _tile_lanes(q_norm[l], NSA_HEADS)
        kg = jnp.concatenate([_tile_lanes(k_norm[l, i], NSA_KV_HEADS) for i in range(N_BRANCH)], axis=1)
        kcg = _tile_lanes(kc_norm[l], NSA_KV_HEADS)
        wab_k, w2_k, pos_k = _compress_weights(cmp_w1_k[l], cmp_w2_k[l], cmp_pos[l])
        wab_v, w2_v, pos_v = _compress_weights(cmp_w1_v[l], cmp_w2_v[l], cmp_pos[l])
        merge_w = (conv_w_p[l], conv_b[l][None], conv_ln_g[l][None], conv_ln_b[l][None], w_out_b[l],
                   ffn_norm[l][None], w_up_b[l], fcw_p[l], ffn_conv_b[l][None], w_down_b[l])

        q, kc, vc, ks, vs, kw, vw, gt, u = _inproj(y_p, attn_norm[l][None], w_cat[l], qg, kg, cos_p, sin_p, bdq,
                                                   PROMPT_TM)
        kcmp = _compress(pt_prompt, kc, wab_k, pos_k, w2_k, kcg, bdk, B, True)
        vcmp = _compress(pt_prompt, vc, wab_v, pos_v, w2_v, kcg, bdk, B, False)
        o = _attn_prompt(q, gt, kcmp, vcmp, ks, vs, kw, vw, ov_p, ex_p, B, S)
        y_p, tail = _merge_ffn(y_p, o, u, conv0, ffn0, *merge_w, FFN_TM, 1)
        for idx, a in enumerate((kc, vc, ks, vs)):
            outs[idx].append(heads(a, (B, S)))
        outs[4].append(heads(kw, (B, S))[:, -p_win:])
        outs[5].append(heads(vw, (B, S))[:, -p_win:])
        outs[6].append(u.reshape(B, S, CONV_CH)[:, -(CONV_WIDTH - 1):])
        outs[7].append(tail[:, -(FFN_CONV_WIDTH - 1):])

        q, kc, vc, ks, vs, kw, vw, gt, u = _inproj(y_s, attn_norm[l][None], w_cat[l], qg, kg, cos_s, sin_s, bdq, TS)
        kcmp = _compress(pt_flat, cache_k_cmp[l], wab_k, pos_k, w2_k, kcg, bdk, DB, True)
        vcmp = _compress(pt_flat, cache_v_cmp[l], wab_v, pos_v, w2_v, kcg, bdk, DB, False)
        kc, vc, ks, vs, kw, vw, u_bm = (to_bm(a) for a in (kc, vc, ks, vs, kw, vw, u))
        kws = state_k_win[l].reshape(DB, win_buf, KV_W)
        vws = state_v_win[l].reshape(DB, win_buf, KV_W)
        o = _attn_sample(pt_flat, to_bm(q), to_bm(gt), kcmp, vcmp, ks, vs, kws, vws, kw, vw, ov_s, ex_s,
                         cache_k_slc[l].reshape(-1, PAGE_SIZE, KV_W), cache_v_slc[l].reshape(-1, PAGE_SIZE, KV_W),
                         past)
        o_tm = o.transpose(1, 0, 2).reshape(TS, NSA_WIDTH)
        cbuf = state_conv[l].transpose(1, 0, 2).reshape(1, (CONV_WIDTH - 1) * DB, CONV_CH)
        fbuf = state_ffn_conv[l].transpose(1, 0, 2).reshape(1, (FFN_CONV_WIDTH - 1) * DB, D_FF)
        y_s, tail = _merge_ffn(y_s, o_tm, u, cbuf, fbuf, *merge_w, TS, DB)
        for idx, a in enumerate((kc, vc, ks, vs)):
            outs[8 + idx].append(heads(a, (DB, DS)))
        outs[12].append(heads(jnp.concatenate([kws, kw], 1)[:, -win_buf:], (DB, win_buf)))
        outs[13].append(heads(jnp.concatenate([vws, vw], 1)[:, -win_buf:], (DB, win_buf)))
        outs[14].append(jnp.concatenate([state_conv[l], u_bm], 1)[:, -(CONV_WIDTH - 1):])
        outs[15].append(tail.reshape(FFN_CONV_WIDTH - 1, DB, D_FF).transpose(1, 0, 2))

    y_prompt = y_p.reshape(B, S, D_MODEL)
    y_sample = y_s.reshape(DS, DB, D_MODEL).transpose(1, 0, 2)
    return (y_prompt, y_sample) + tuple(jnp.stack(o) for o in outs)
```

```python
import functools

import jax
import jax.numpy as jnp
from jax import lax
from jax.experimental import pallas as pl
from jax.experimental.pallas import tpu as pltpu

F32 = jnp.float32
BF16 = jnp.bfloat16

D_MODEL = 1024
HEAD_DIM = 64
NSA_HEADS = 8
NSA_KV_HEADS = 2
NSA_GROUP = NSA_HEADS // NSA_KV_HEADS
NSA_WIDTH = NSA_HEADS * HEAD_DIM
KV_W = NSA_KV_HEADS * HEAD_DIM
CONV_CH = D_MODEL - NSA_WIDTH
N_BRANCH = 3
CMP_BLOCK = 32
CMP_STRIDE = 16
CMP_HIDDEN = 128
SEL_BLOCK = 64
N_SELECT = 16
WINDOW = 512
CONV_WIDTH = 31
D_FF = 2816
FFN_CONV_WIDTH = 3
ROPE_THETA = 10000.0
EPS = 1e-6
Q_BLOCK = 128
GATE_COLS = NSA_HEADS * N_BRANCH
PAGE_SIZE = 128
SCALE = HEAD_DIM ** -0.5
NEG = -1e30

LANES = 128
SUBLANES = 8
VMEM_LIMIT = 52 * 1024 * 1024
CHUNK_W = CMP_STRIDE * KV_W
PAGE_CHUNKS = PAGE_SIZE // CMP_STRIDE
Z_COLS = NSA_WIDTH + 6 * KV_W + LANES + 2 * CONV_CH

assert NSA_KV_HEADS == 2 and KV_W == LANES


def _round_up(x, m):
    return (x + m - 1) // m * m


def _bdot(a, b):
    return jnp.dot(a.astype(BF16), b.astype(BF16), preferred_element_type=F32)


def _bdot_nt(a, b):
    return lax.dot_general(a.astype(BF16), b.astype(BF16), (((1,), (1,)), ((), ())),
                           preferred_element_type=F32)


def _split(a):
    hi = a.astype(BF16)
    return hi, (a - hi.astype(F32)).astype(BF16)


def _split_dot(a, b_bf16):
    hi, lo = _split(a)
    return (jnp.dot(hi, b_bf16, preferred_element_type=F32)
            + jnp.dot(lo, b_bf16, preferred_element_type=F32))


def _head_rms(x, bd, gain):
    ms = _split_dot(x * x, bd) * (1.0 / HEAD_DIM)
    return x * lax.rsqrt(ms + EPS) * gain


def _rope(y, cos, sin_signed):
    w = y.shape[-1]
    half = HEAD_DIM // 2
    fwd = pltpu.roll(y, w - half, 1)
    bwd = pltpu.roll(y, half, 1)
    lane = lax.broadcasted_iota(jnp.int32, y.shape, 1)
    swapped = jnp.where((lane & (HEAD_DIM - 1)) < half, fwd, bwd)
    return y * cos + swapped * sin_signed


def _masked_softmax(s, mask):
    s = jnp.where(mask, s, NEG)
    m = jnp.max(s, -1, keepdims=True)
    e = jnp.where(mask, jnp.exp(s - m), 0.0)
    return e / jnp.maximum(jnp.sum(e, -1, keepdims=True), 1e-30)


def _const_spec(shape):
    nd = len(shape)
    return pl.BlockSpec(shape, lambda *_: (0,) * nd, pipeline_mode=pl.Buffered(1))


def _inproj_kernel(x_ref, g_ref, w_ref, qg_ref, kg_ref, cos_ref, sin_ref, bd_ref,
                   q_ref, kc_ref, vc_ref, ks_ref, vs_ref, kw_ref, vw_ref, gt_ref, u_ref):
    x = x_ref[...]
    h = x * lax.rsqrt(jnp.mean(x * x, -1, keepdims=True) + EPS) * g_ref[...]
    z = jnp.dot(h.astype(BF16), w_ref[...], preferred_element_type=F32)
    cos = cos_ref[...]
    sin = sin_ref[...]
    bdq = bd_ref[...]
    bdk = bdq[:KV_W, :KV_W]
    nrep = NSA_WIDTH // KV_W
    q = _head_rms(z[:, :NSA_WIDTH], bdq, qg_ref[...])
    q_ref[...] = _rope(q, jnp.concatenate([cos] * nrep, axis=1), jnp.concatenate([sin] * nrep, axis=1))
    kg = kg_ref[...]
    o = NSA_WIDTH
    kc_ref[...] = _rope(_head_rms(z[:, o:o + KV_W], bdk, kg[:, 0:KV_W]), cos, sin)
    vc_ref[...] = z[:, o + KV_W:o + 2 * KV_W]
    ks_ref[...] = _rope(_head_rms(z[:, o + 2 * KV_W:o + 3 * KV_W], bdk, kg[:, KV_W:2 * KV_W]), cos, sin)
    vs_ref[...] = z[:, o + 3 * KV_W:o + 4 * KV_W]
    kw_ref[...] = _rope(_head_rms(z[:, o + 4 * KV_W:o + 5 * KV_W], bdk, kg[:, 2 * KV_W:3 * KV_W]), cos, sin)
    vw_ref[...] = z[:, o + 5 * KV_W:o + 6 * KV_W]
    o += 6 * KV_W
    gt_ref[...] = jax.nn.sigmoid(z[:, o:o + LANES])
    o += LANES
    u_ref[...] = z[:, o:o + CONV_CH] * jax.nn.sigmoid(z[:, o + CONV_CH:o + 2 * CONV_CH])


def _inproj(x, g, w, qg, kg, cos, sin, bd, tm):
    t = x.shape[0]
    nt = t // tm
    npos = cos.shape[0] // tm
    row = lambda i: (i, 0)
    pos = lambda i: (i % npos, 0)
    widths = [NSA_WIDTH] + [KV_W] * 6 + [LANES, CONV_CH]
    return pl.pallas_call(
        _inproj_kernel,
        grid=(nt,),
        in_specs=[pl.BlockSpec((tm, D_MODEL), row), _const_spec((1, D_MODEL)), _const_spec((D_MODEL, Z_COLS)),
                  _const_spec((1, NSA_WIDTH)), _const_spec((1, 3 * KV_W)),
                  pl.BlockSpec((tm, KV_W), pos), pl.BlockSpec((tm, KV_W), pos),
                  _const_spec((NSA_WIDTH, NSA_WIDTH))],
        out_specs=[pl.BlockSpec((tm, wd), row) for wd in widths],
        out_shape=[jax.ShapeDtypeStruct((t, wd), F32) for wd in widths],
        compiler_params=pltpu.CompilerParams(dimension_semantics=("arbitrary",), vmem_limit_bytes=VMEM_LIMIT),
        name="inproj",
    )(x, g, w, qg, kg, cos, sin, bd)


def _page_copies(pt_ref, pool_ref, buf, sem, batch, slot, n_pages, rows_per_page):
    return [pltpu.make_async_copy(pool_ref.at[pt_ref[batch * n_pages + p]],
                                  buf.at[slot, pl.ds(p * rows_per_page, rows_per_page)],
                                  sem.at[slot])
            for p in range(n_pages)]


def _compress_kernel(pt_ref, pool_ref, wab_ref, pos_ref, w2_ref, g_ref, bd_ref, out_ref, buf, sem,
                     *, n_pages, norm):
    b = pl.program_id(0)
    nb = pl.num_programs(0)
    copies = functools.partial(_page_copies, pt_ref, pool_ref, buf, sem,
                               n_pages=n_pages, rows_per_page=PAGE_CHUNKS)

    @pl.when(b == 0)
    def _():
        for c in copies(0, 0):
            c.start()

    @pl.when(b + 1 < nb)
    def _():
        for c in copies(b + 1, (b + 1) % 2):
            c.start()

    slot = b % 2
    for c in copies(b, slot):
        c.wait()

    x = buf[slot]
    nch = x.shape[0]
    hid2 = 2 * CMP_HIDDEN
    wab = wab_ref[...]
    z = jnp.dot(x.astype(BF16), wab, preferred_element_type=F32)
    pos = pos_ref[...].astype(BF16)
    bias = (jnp.dot(pos[0:SUBLANES], wab[:, :hid2], preferred_element_type=F32)
            + jnp.dot(pos[SUBLANES:], wab[:, hid2:], preferred_element_type=F32))[0:1]
    nxt = pltpu.roll(z[:, hid2:], nch - 1, 0)
    hid = jax.nn.gelu(z[:, :hid2] + nxt + bias)
    y = jnp.dot(hid.astype(BF16), w2_ref[...], preferred_element_type=F32)
    if norm:
        y = _head_rms(y, bd_ref[...], g_ref[...])
    rowi = lax.broadcasted_iota(jnp.int32, y.shape, 0)
    out_ref[0] = jnp.where(rowi < nch - 1, y, 0.0)


def _compress(page_table, pool3, wab, pos, w2, g, bd, n_batch, norm):
    n_pages = page_table.shape[0] // n_batch
    nch = n_pages * PAGE_CHUNKS
    grid_spec = pltpu.PrefetchScalarGridSpec(
        num_scalar_prefetch=1,
        grid=(n_batch,),
        in_specs=[pl.BlockSpec(memory_space=pl.ANY),
                  pl.BlockSpec(wab.shape, lambda b, pt: (0, 0)),
                  pl.BlockSpec(pos.shape, lambda b, pt: (0, 0)),
                  pl.BlockSpec(w2.shape, lambda b, pt: (0, 0)),
                  pl.BlockSpec(g.shape, lambda b, pt: (0, 0)),
                  pl.BlockSpec(bd.shape, lambda b, pt: (0, 0))],
        out_specs=pl.BlockSpec((1, nch, KV_W), lambda b, pt: (b, 0, 0)),
        scratch_shapes=[pltpu.VMEM((2, nch, CHUNK_W), F32), pltpu.SemaphoreType.DMA((2,))],
    )
    return pl.pallas_call(
        functools.partial(_compress_kernel, n_pages=n_pages, norm=norm),
        grid_spec=grid_spec,
        out_shape=jax.ShapeDtypeStruct((n_batch, nch, KV_W), F32),
        compiler_params=pltpu.CompilerParams(dimension_semantics=("arbitrary",), vmem_limit_bytes=VMEM_LIMIT),
        name="compress",
    )(page_table, pool3, wab, pos, w2, g, bd)


def _select_lanes(imp, cur, n_blocks):
    n = lax.broadcasted_iota(jnp.int32, (1, imp.shape[1]), 1)
    forced = (n == 0) | (n == cur) | (n == cur - 1)
    v = jnp.where(forced, jnp.inf, jnp.where(n <= cur, imp, -jnp.inf))
    rank = jnp.zeros(imp.shape, jnp.int32)
    for m in range(n_blocks):
        vm = v[:, m:m + 1]
        before = (n > m).astype(jnp.int32)
        rank = rank + jnp.where(vm > v, 1, jnp.where(vm == v, before, 0))
    return jnp.where(rank < min(N_SELECT, n_blocks), 1.0, 0.0).astype(BF16)


def _select_rows(imp_t, cur, n_blocks):
    n = lax.broadcasted_iota(jnp.int32, imp_t.shape, 0)
    forced = (n == 0) | (n == cur) | (n == cur - 1)
    v = jnp.where(forced, jnp.inf, jnp.where(n <= cur, imp_t, -jnp.inf))
    rank = jnp.zeros(imp_t.shape, jnp.int32)
    for m in range(n_blocks):
        vm = v[m:m + 1, :]
        rank = rank + jnp.where(vm > v, 1, jnp.where(vm == v, (n > m).astype(jnp.int32), 0))
    return jnp.where(rank < min(N_SELECT, n_blocks), 1.0, 0.0)


KEY_BLOCK = 512


def _attn_prompt_kernel(q_ref, gt_ref, kcmp_ref, vcmp_ref, ks_ref, vs_ref, kw_ref, vw_ref, ovt_ref, oh_ref, gx_ref,
                        o_ref, ksa, vsb, kwb, vwb, kcb, vcb, *, seq):
    i = pl.program_id(1)

    @pl.when(i == 0)
    def _():
        ksa[:, 0:KV_W] = ks_ref[0].astype(BF16)
        ksa[:, KV_W:2 * KV_W] = oh_ref[...]
        vsb[...] = vs_ref[0].astype(BF16)
        kwb[...] = kw_ref[0].astype(BF16)
        vwb[...] = vw_ref[0].astype(BF16)
        kcb[...] = kcmp_ref[0].astype(BF16)
        vcb[...] = vcmp_ref[0].astype(BF16)

    qs = i * Q_BLOCK
    q = q_ref[...] * SCALE
    rows = NSA_GROUP * Q_BLOCK
    lane = lax.broadcasted_iota(jnp.int32, (1, KV_W), 1)
    qpos4 = qs + (lax.broadcasted_iota(jnp.int32, (rows, 1), 0) & (Q_BLOCK - 1))
    qpos_l = qs + lax.broadcasted_iota(jnp.int32, (1, Q_BLOCK), 1)
    n_cmp = kcb.shape[0]
    n_blocks = seq // SEL_BLOCK
    win_keys = WINDOW + Q_BLOCK
    n_kb = (qs + Q_BLOCK + KEY_BLOCK - 1) // KEY_BLOCK
    w0 = pl.multiple_of(jnp.maximum(qs - WINDOW, 0), Q_BLOCK)
    nt_dims = (((1,), (1,)), ((), ()))
    o_c, o_s, o_w = [], [], []
    for h in range(NSA_KV_HEADS):
        in_head = (lane >= h * HEAD_DIM) & (lane < (h + 1) * HEAD_DIM)
        q4 = jnp.concatenate([jnp.where(in_head, q[:, g * KV_W:(g + 1) * KV_W], 0.0) for g in range(NSA_GROUP)],
                             axis=0).astype(BF16)
        s_c = lax.dot_general(q4, kcb[...], nt_dims, preferred_element_type=F32)
        c = lax.broadcasted_iota(jnp.int32, (1, n_cmp), 1)
        p_c = _masked_softmax(s_c, (c * CMP_STRIDE + (CMP_BLOCK - 1)) <= qpos4)
        o_c.append(jnp.dot(p_c.astype(BF16), vcb[...], preferred_element_type=F32))
        p_sum = p_c[0:Q_BLOCK]
        for g in range(1, NSA_GROUP):
            p_sum = p_sum + p_c[g * Q_BLOCK:(g + 1) * Q_BLOCK]
        hi, lo = _split(p_sum)
        ovt = ovt_ref[...]
        imp_t = (lax.dot_general(ovt, hi, nt_dims, preferred_element_type=F32)
                 + lax.dot_general(ovt, lo, nt_dims, preferred_element_type=F32))[0:n_blocks]
        sel_t = _select_rows(imp_t, qpos_l // SEL_BLOCK, n_blocks)
        bias_t = jnp.concatenate([jnp.where(sel_t > 0.5, 0.0, NEG),
                                  jnp.zeros((KV_W - n_blocks, Q_BLOCK), F32)], axis=0)
        bias = bias_t.T.astype(BF16)
        q4a = jnp.concatenate([q4, jnp.concatenate([bias] * NSA_GROUP, axis=0)], axis=1)

        def sel_block(j, carry, causal):
            m_i, l_i, acc = carry
            k0 = pl.multiple_of(j * KEY_BLOCK, KEY_BLOCK)
            s = lax.dot_general(q4a, ksa[pl.ds(k0, KEY_BLOCK), :], nt_dims, preferred_element_type=F32)
            if causal:
                tok = k0 + lax.broadcasted_iota(jnp.int32, (1, KEY_BLOCK), 1)
                s = jnp.where(tok <= qpos4, s, NEG)
            m_new = jnp.maximum(m_i, jnp.max(s, -1, keepdims=True))
            alpha = jnp.exp(m_i - m_new)
            e = jnp.exp(s - m_new)
            l_new = alpha * l_i + jnp.sum(e, -1, keepdims=True)
            acc_new = alpha * acc + jnp.dot(e.astype(BF16), vsb[pl.ds(k0, KEY_BLOCK), :],
                                            preferred_element_type=F32)
            return m_new, l_new, acc_new

        carry = (jnp.full((rows, 1), NEG, F32), jnp.zeros((rows, 1), F32), jnp.zeros((rows, KV_W), F32))
        carry = lax.fori_loop(0, n_kb - 1, functools.partial(sel_block, causal=False), carry)
        _, l_s, acc_s = sel_block(n_kb - 1, carry, True)
        o_s.append(acc_s * (1.0 / l_s))
        s_w = lax.dot_general(q4, kwb[pl.ds(w0, win_keys), :], nt_dims, preferred_element_type=F32)
        dp = qpos4 - (w0 + lax.broadcasted_iota(jnp.int32, (1, win_keys), 1))
        s_w = jnp.where((dp >= 0) & (dp < WINDOW), s_w, NEG)
        e_w = jnp.exp(s_w - jnp.max(s_w, -1, keepdims=True))
        acc_w = jnp.dot(e_w.astype(BF16), vwb[pl.ds(w0, win_keys), :], preferred_element_type=F32)
        o_w.append(acc_w * (1.0 / jnp.sum(e_w, -1, keepdims=True)))

    first = lane < HEAD_DIM
    ghi, glo = _split(gt_ref[...])
    out = jnp.zeros((Q_BLOCK, NSA_WIDTH), F32)
    for br, parts in enumerate((o_c, o_s, o_w)):
        gx = gx_ref[br]
        gate = jnp.dot(ghi, gx, preferred_element_type=F32) + jnp.dot(glo, gx, preferred_element_type=F32)
        merged = jnp.concatenate([jnp.where(first, parts[0][g * Q_BLOCK:(g + 1) * Q_BLOCK],
                                            parts[1][g * Q_BLOCK:(g + 1) * Q_BLOCK])
                                  for g in range(NSA_GROUP)], axis=1)
        out = out + gate * merged
    o_ref[...] = out


def _attn_prompt(q, gt, kcmp, vcmp, ks, vs, kw, vw, ovt, oh, gx, n_batch, seq):
    nq = seq // Q_BLOCK
    n_cmp = kcmp.shape[1]
    tok = lambda b, i: (b * nq + i, 0)
    per_b = lambda b, i: (b, 0, 0)
    kv3 = lambda a: a.reshape(n_batch, seq, KV_W)
    return pl.pallas_call(
        functools.partial(_attn_prompt_kernel, seq=seq),
        grid=(n_batch, nq),
        in_specs=[pl.BlockSpec((Q_BLOCK, NSA_WIDTH), tok), pl.BlockSpec((Q_BLOCK, LANES), tok),
                  pl.BlockSpec((1, n_cmp, KV_W), per_b), pl.BlockSpec((1, n_cmp, KV_W), per_b),
                  pl.BlockSpec((1, seq, KV_W), per_b), pl.BlockSpec((1, seq, KV_W), per_b),
                  pl.BlockSpec((1, seq, KV_W), per_b), pl.BlockSpec((1, seq, KV_W), per_b),
                  _const_spec(ovt.shape), _const_spec(oh.shape), _const_spec(gx.shape)],
        out_specs=pl.BlockSpec((Q_BLOCK, NSA_WIDTH), tok),
        out_shape=jax.ShapeDtypeStruct((n_batch * seq, NSA_WIDTH), F32),
        scratch_shapes=[pltpu.VMEM((seq, 2 * KV_W), BF16), pltpu.VMEM((seq, KV_W), BF16),
                        pltpu.VMEM((seq, KV_W), BF16), pltpu.VMEM((seq, KV_W), BF16),
                        pltpu.VMEM((n_cmp, KV_W), BF16), pltpu.VMEM((n_cmp, KV_W), BF16)],
        compiler_params=pltpu.CompilerParams(dimension_semantics=("arbitrary", "arbitrary"),
                                             vmem_limit_bytes=VMEM_LIMIT),
        name="attn_prompt",
    )(q, gt, kcmp, vcmp, kv3(ks), kv3(vs), kv3(kw), kv3(vw), ovt, oh, gx)


def _attn_sample_kernel(pt_ref, q_ref, gt_ref, kcmp_ref, vcmp_ref, ksn_ref, vsn_ref, kws_ref, vws_ref,
                        kwn_ref, vwn_ref, ov_ref, ex_ref, kpool_ref, vpool_ref,
                        o_ref, kbuf, vbuf, kwbuf, vwbuf, ksem, vsem, *, n_pages, past):
    b = pl.program_id(0)
    nb = pl.num_programs(0)
    kcopies = functools.partial(_page_copies, pt_ref, kpool_ref, kbuf, ksem,
                                n_pages=n_pages, rows_per_page=PAGE_SIZE)
    vcopies = functools.partial(_page_copies, pt_ref, vpool_ref, vbuf, vsem,
                                n_pages=n_pages, rows_per_page=PAGE_SIZE)

    @pl.when(b == 0)
    def _():
        for c in kcopies(0, 0) + vcopies(0, 0):
            c.start()

    @pl.when(b + 1 < nb)
    def _():
        for c in kcopies(b + 1, (b + 1) % 2) + vcopies(b + 1, (b + 1) % 2):
            c.start()

    slot = b % 2
    ds = q_ref.shape[1]
    keys = kbuf.shape[1]
    kbuf[slot, pl.ds(past, ds), :] = ksn_ref[0]
    vbuf[slot, pl.ds(past, ds), :] = vsn_ref[0]
    pad = jnp.zeros((keys - past - ds, KV_W), F32)
    kbuf[slot, pl.ds(past + ds, keys - past - ds), :] = pad
    vbuf[slot, pl.ds(past + ds, keys - past - ds), :] = pad
    nwin = kws_ref.shape[1]
    wkeys = kwbuf.shape[0]
    kwbuf[pl.ds(0, nwin), :] = kws_ref[0]
    vwbuf[pl.ds(0, nwin), :] = vws_ref[0]
    kwbuf[pl.ds(nwin, ds), :] = kwn_ref[0]
    vwbuf[pl.ds(nwin, ds), :] = vwn_ref[0]
    wpad = jnp.zeros((wkeys - nwin - ds, KV_W), F32)
    kwbuf[pl.ds(nwin + ds, wkeys - nwin - ds), :] = wpad
    vwbuf[pl.ds(nwin + ds, wkeys - nwin - ds), :] = wpad

    q = q_ref[0] * SCALE
    gt = gt_ref[0]
    lane = lax.broadcasted_iota(jnp.int32, (1, KV_W), 1)
    blocks = []
    for h in range(NSA_KV_HEADS):
        in_head = (lane >= h * HEAD_DIM) & (lane < (h + 1) * HEAD_DIM)
        for g in range(NSA_GROUP):
            blocks.append(jnp.where(in_head, q[:, g * KV_W:(g + 1) * KV_W], 0.0))
    qp = jnp.concatenate(blocks, axis=0).astype(BF16)
    rows = qp.shape[0]
    per_head = NSA_GROUP * ds
    qpos = past + (lax.broadcasted_iota(jnp.int32, (rows, 1), 0) & (ds - 1))

    n_cmp = kcmp_ref.shape[1]
    s_c = _bdot_nt(qp, kcmp_ref[0])
    c = lax.broadcasted_iota(jnp.int32, (1, n_cmp), 1)
    p_c = _masked_softmax(s_c, (c * CMP_STRIDE + (CMP_BLOCK - 1)) <= qpos)
    o_c = _bdot(p_c, vcmp_ref[0])
    sums = []
    for h in range(NSA_KV_HEADS):
        acc = p_c[h * per_head:h * per_head + ds]
        for g in range(1, NSA_GROUP):
            acc = acc + p_c[h * per_head + g * ds:h * per_head + (g + 1) * ds]
        sums.append(acc)
    imp = _split_dot(jnp.concatenate(sums, axis=0), ov_ref[...])
    qpos_sel = past + (lax.broadcasted_iota(jnp.int32, (NSA_KV_HEADS * ds, 1), 0) & (ds - 1))
    n_blocks = -(-(past + ds) // SEL_BLOCK)
    sel = _select_lanes(imp, qpos_sel // SEL_BLOCK, n_blocks)
    sel = jnp.concatenate([sel[h * ds:(h + 1) * ds] for h in range(NSA_KV_HEADS) for _ in range(NSA_GROUP)],
                          axis=0)

    for c_ in kcopies(b, slot) + vcopies(b, slot):
        c_.wait()

    selm = jnp.dot(sel, ex_ref[...], preferred_element_type=F32)
    s_s = _bdot_nt(qp, kbuf[slot])
    tok = lax.broadcasted_iota(jnp.int32, (1, keys), 1)
    p_s = _masked_softmax(s_s, (selm > 0.5) & (tok <= qpos))
    o_s = _bdot(p_s, vbuf[slot])
    s_w = _bdot_nt(qp, kwbuf[...])
    dp = qpos - (past - nwin + lax.broadcasted_iota(jnp.int32, (1, wkeys), 1))
    p_w = _masked_softmax(s_w, (dp >= 0) & (dp < WINDOW))
    o_w = _bdot(p_w, vwbuf[...])

    first = lane < HEAD_DIM
    tiles = []
    for g in range(NSA_GROUP):
        halves = []
        for h in range(NSA_KV_HEADS):
            j = (h * NSA_GROUP + g) * N_BRANCH
            r = slice(h * per_head + g * ds, h * per_head + (g + 1) * ds)
            halves.append(gt[:, j:j + 1] * o_c[r] + gt[:, j + 1:j + 2] * o_s[r] + gt[:, j + 2:j + 3] * o_w[r])
        tiles.append(jnp.where(first, halves[0], halves[1]))
    o_ref[0] = jnp.concatenate(tiles, axis=1)


def _attn_sample(page_table, q, gt, kcmp, vcmp, ksn, vsn, kws, vws, kwn, vwn, ov, ex, kpool, vpool, past):
    n_batch, ds = q.shape[:2]
    n_pages = page_table.shape[0] // n_batch
    n_cmp = kcmp.shape[1]
    nwin = kws.shape[1]
    keys = ex.shape[1]
    wkeys = _round_up(nwin + ds, LANES)
    per_b = lambda b, pt: (b, 0, 0)
    full = lambda a: pl.BlockSpec(a.shape, lambda b, pt: (0,) * a.ndim)
    grid_spec = pltpu.PrefetchScalarGridSpec(
        num_scalar_prefetch=1,
        grid=(n_batch,),
        in_specs=[pl.BlockSpec((1, ds, NSA_WIDTH), per_b), pl.BlockSpec((1, ds, LANES), per_b),
                  pl.BlockSpec((1, n_cmp, KV_W), per_b), pl.BlockSpec((1, n_cmp, KV_W), per_b),
                  pl.BlockSpec((1, ds, KV_W), per_b), pl.BlockSpec((1, ds, KV_W), per_b),
                  pl.BlockSpec((1, nwin, KV_W), per_b), pl.BlockSpec((1, nwin, KV_W), per_b),
                  pl.BlockSpec((1, ds, KV_W), per_b), pl.BlockSpec((1, ds, KV_W), per_b),
                  full(ov), full(ex),
                  pl.BlockSpec(memory_space=pl.ANY), pl.BlockSpec(memory_space=pl.ANY)],
        out_specs=pl.BlockSpec((1, ds, NSA_WIDTH), per_b),
        scratch_shapes=[pltpu.VMEM((2, keys, KV_W), F32), pltpu.VMEM((2, keys, KV_W), F32),
                        pltpu.VMEM((wkeys, KV_W), F32), pltpu.VMEM((wkeys, KV_W), F32),
                        pltpu.SemaphoreType.DMA((2,)), pltpu.SemaphoreType.DMA((2,))],
    )
    return pl.pallas_call(
        functools.partial(_attn_sample_kernel, n_pages=n_pages, past=past),
        grid_spec=grid_spec,
        out_shape=jax.ShapeDtypeStruct((n_batch, ds, NSA_WIDTH), F32),
        compiler_params=pltpu.CompilerParams(dimension_semantics=("arbitrary",), vmem_limit_bytes=VMEM_LIMIT),
        name="attn_sample",
    )(page_table, q, gt, kcmp, vcmp, ksn, vsn, kws, vws, kwn, vwn, ov, ex, kpool, vpool)


CONV_ROWS = 64
FF_SPLIT = 2


def _conv_shifts(hpc, rs):
    off = hpc - (CONV_WIDTH - 1) * rs
    return sorted({(off + k * rs) % SUBLANES for k in range(CONV_WIDTH)} - {0})


def _merge_ffn_kernel(x_ref, o_ref, u_ref, cbuf_ref, fbuf_ref, cw_ref, cb_ref, lng_ref, lnb_ref, wout_ref,
                      fg_ref, wup_ref, fcw_ref, fcb_ref, wdn_ref,
                      y_ref, tail_ref, xx, xsh, cc, gg, *, rs):
    t = pl.program_id(1)
    tm = x_ref.shape[0]
    hpc = xx.shape[0] - tm
    hpf = gg.shape[0] - tm
    off_c = hpc - (CONV_WIDTH - 1) * rs
    off_f = hpf - (FFN_CONV_WIDTH - 1) * rs

    @pl.when(t == 0)
    def _():
        xx[pl.ds(0, hpc), :] = cbuf_ref[0]
        gg[pl.ds(0, hpf), :] = fbuf_ref[0]

    xx[pl.ds(hpc, tm), :] = u_ref[...]
    shifts = _conv_shifts(hpc, rs)
    span = hpc + tm - SUBLANES
    for idx, r in enumerate(shifts):
        xsh[idx, pl.ds(0, span), :] = xx[pl.ds(r, span), :]
    rc = min(CONV_ROWS, tm)
    for c0 in range(0, CONV_CH, LANES):
        for r0 in range(0, tm, rc):
            acc = jnp.zeros((rc, LANES), F32)
            for k in range(CONV_WIDTH):
                start = off_c + k * rs
                r = start % SUBLANES
                if r == 0:
                    tap = xx[pl.ds(start + r0, rc), c0:c0 + LANES]
                else:
                    tap = xsh[shifts.index(r), pl.ds(start - r + r0, rc), c0:c0 + LANES]
                acc = acc + cw_ref[k:k + 1, c0:c0 + LANES] * tap
            cc[pl.ds(r0, rc), c0:c0 + LANES] = acc
    y = cc[...] + cb_ref[...]
    mu = jnp.mean(y, -1, keepdims=True)
    var = jnp.mean(jnp.square(y - mu), -1, keepdims=True)
    y = (y - mu) * lax.rsqrt(var + EPS) * lng_ref[...] + lnb_ref[...]
    c = y * jax.nn.sigmoid(y)
    x1 = (x_ref[...] + _bdot(o_ref[...], wout_ref[0:NSA_WIDTH, :])
          + _bdot(c, wout_ref[NSA_WIDTH:NSA_WIDTH + CONV_CH, :]))
    h = (x1 * lax.rsqrt(jnp.mean(x1 * x1, -1, keepdims=True) + EPS) * fg_ref[...]).astype(BF16)
    fw = D_FF // FF_SPLIT
    f = jnp.zeros((tm, D_MODEL), F32)
    for s in range(FF_SPLIT):
        c0 = s * fw
        gg[pl.ds(hpf, tm), c0:c0 + fw] = jnp.dot(h, wup_ref[:, c0:c0 + fw], preferred_element_type=F32)
        up = jnp.dot(h, wup_ref[:, D_FF + c0:D_FF + c0 + fw], preferred_element_type=F32)
        gc = fcb_ref[:, c0:c0 + fw]
        for k in range(FFN_CONV_WIDTH):
            gc = gc + fcw_ref[k:k + 1, c0:c0 + fw] * gg[pl.ds(off_f + k * rs, tm), c0:c0 + fw]
        act = gc * jax.nn.sigmoid(gc) * up
        f = f + jnp.dot(act.astype(BF16), wdn_ref[c0:c0 + fw, :], preferred_element_type=F32)
    y_ref[...] = x1 + f
    tail_ref[0] = gg[pl.ds(tm, hpf), :]
    xx[pl.ds(0, hpc), :] = xx[pl.ds(tm, hpc), :]
    gg[pl.ds(0, hpf), :] = gg[pl.ds(tm, hpf), :]


def _merge_ffn(x, o, u, cbuf, fbuf, cw, cb, lng, lnb, wout, fg, wup, fcw, fcb, wdn, tm, rs):
    t = x.shape[0]
    n_batch = cbuf.shape[0]
    nt = t // n_batch // tm
    hpc = cbuf.shape[1]
    hpf = fbuf.shape[1]
    n_shift = len(_conv_shifts(hpc, rs))
    xsh_shape = (n_shift, hpc + tm - SUBLANES, CONV_CH) if n_shift else (1, SUBLANES, LANES)
    tok = lambda b, i: (b * nt + i, 0)
    per_b = lambda b, i: (b, 0, 0)
    return pl.pallas_call(
        functools.partial(_merge_ffn_kernel, rs=rs),
        grid=(n_batch, nt),
        in_specs=[pl.BlockSpec((tm, D_MODEL), tok), pl.BlockSpec((tm, NSA_WIDTH), tok),
                  pl.BlockSpec((tm, CONV_CH), tok),
                  pl.BlockSpec((1, hpc, CONV_CH), per_b), pl.BlockSpec((1, hpf, D_FF), per_b),
                  _const_spec(cw.shape), _const_spec(cb.shape), _const_spec(lng.shape), _const_spec(lnb.shape),
                  _const_spec(wout.shape), _const_spec(fg.shape), _const_spec(wup.shape),
                  _const_spec(fcw.shape), _const_spec(fcb.shape), _const_spec(wdn.shape)],
        out_specs=[pl.BlockSpec((tm, D_MODEL), tok), pl.BlockSpec((1, hpf, D_FF), per_b)],
        out_shape=[jax.ShapeDtypeStruct((t, D_MODEL), F32), jax.ShapeDtypeStruct((n_batch, hpf, D_FF), F32)],
        scratch_shapes=[pltpu.VMEM((hpc + tm, CONV_CH), F32), pltpu.VMEM(xsh_shape, F32),
                        pltpu.VMEM((tm, CONV_CH), F32), pltpu.VMEM((hpf + tm, D_FF), F32)],
        compiler_params=pltpu.CompilerParams(dimension_semantics=("arbitrary", "arbitrary"),
                                             vmem_limit_bytes=VMEM_LIMIT),
        name="merge_ffn",
    )(x, o, u, cbuf, fbuf, cw, cb, lng, lnb, wout, fg, wup, fcw, fcb, wdn)


def _rope_tables(pos):
    half = HEAD_DIM // 2
    inv = ROPE_THETA ** (-jnp.arange(half, dtype=F32) / half)
    ang = pos.astype(F32)[:, None] * inv[None, :]
    cos, sin = jnp.cos(ang), jnp.sin(ang)
    reps = KV_W // HEAD_DIM
    return (jnp.tile(jnp.concatenate([cos, cos], -1), (1, reps)),
            jnp.tile(jnp.concatenate([-sin, sin], -1), (1, reps)))


def _head_sum_matrix(width):
    i = jnp.arange(width) // HEAD_DIM
    return (i[:, None] == i[None, :]).astype(BF16)


def _overlap_matrix(n_cmp_rows, n_blocks, n_lanes):
    c0 = jnp.arange(n_cmp_rows) * CMP_STRIDE
    s0 = jnp.arange(n_lanes) * SEL_BLOCK
    ov = jnp.clip(jnp.minimum(c0[:, None] + CMP_BLOCK, s0[None, :] + SEL_BLOCK)
                  - jnp.maximum(c0[:, None], s0[None, :]), 0, CMP_BLOCK).astype(F32) / CMP_BLOCK
    ov = jnp.where(jnp.arange(n_lanes)[None, :] < n_blocks, ov, 0.0)
    return ov.astype(BF16)


def _expand_matrix(n_lanes, n_blocks, n_keys):
    n = jnp.arange(n_lanes)[:, None]
    t = jnp.arange(n_keys)[None, :] // SEL_BLOCK
    return ((n == t) & (n < n_blocks)).astype(BF16)


def _gate_expand_matrix():
    col = jnp.arange(NSA_WIDTH)
    head = ((col % KV_W) // HEAD_DIM) * NSA_GROUP + col // KV_W
    j = jnp.arange(LANES)[None, :, None]
    br = jnp.arange(N_BRANCH)[:, None, None]
    return (j == head[None, None, :] * N_BRANCH + br).astype(BF16)


def _pair_order(a, axis):
    shape = a.shape
    a = a.reshape(shape[:axis] + (NSA_KV_HEADS, NSA_GROUP, HEAD_DIM) + shape[axis + 1:])
    return jnp.swapaxes(a, axis, axis + 1).reshape(shape)


def _compress_weights(w1, w2, pos):
    half = CMP_STRIDE * HEAD_DIM
    eye = jnp.eye(NSA_KV_HEADS, dtype=F32)

    def spread(wpart):
        w3 = wpart.reshape(CMP_STRIDE, HEAD_DIM, CMP_HIDDEN)
        return jnp.einsum("tdh,kj->tkdjh", w3, eye).reshape(CHUNK_W, NSA_KV_HEADS * CMP_HIDDEN)

    wab = jnp.concatenate([spread(w1[:half]), spread(w1[half:])], axis=1).astype(BF16)
    w2bd = jnp.einsum("hd,kj->khjd", w2, eye).reshape(NSA_KV_HEADS * CMP_HIDDEN, KV_W).astype(BF16)

    def pos_row(p):
        row = jnp.tile(p[:, None, :], (1, NSA_KV_HEADS, 1)).reshape(1, CHUNK_W)
        return jnp.concatenate([row, jnp.zeros((SUBLANES - 1, CHUNK_W), F32)], axis=0)

    posab = jnp.concatenate([pos_row(pos[:CMP_STRIDE]), pos_row(pos[CMP_STRIDE:])], axis=0)
    return wab, w2bd, posab


def _tile_lanes(v, reps):
    return jnp.tile(v, reps)[None, :]


PROMPT_TM = 512
FFN_TM = 256


def kernel(x_prompt, x_sample, cache_k_cmp, cache_v_cmp, cache_k_slc, cache_v_slc, state_k_win, state_v_win,
           state_conv, state_ffn_conv, page_table, attn_norm, w_in, q_norm, k_norm, kc_norm, cmp_pos,
           cmp_w1_k, cmp_w2_k, cmp_w1_v, cmp_w2_v, conv_w, conv_b, conv_ln_g, conv_ln_b, w_out, ffn_norm,
           w_up, ffn_conv_w, ffn_conv_b, w_down):
    B, S = x_prompt.shape[:2]
    DB, DS = x_sample.shape[:2]
    depth = w_in.shape[0]
    n_pool = cache_k_cmp.shape[1]
    n_pages = page_table.shape[1]
    past = n_pages * PAGE_SIZE
    win_buf = state_k_win.shape[2]
    p_win = min(WINDOW, S)
    TS = DB * DS

    cos_p, sin_p = _rope_tables(jnp.arange(S))
    cos_s, sin_s = _rope_tables(jnp.repeat(past + jnp.arange(DS), DB))
    bdq = _head_sum_matrix(NSA_WIDTH)
    bdk = _head_sum_matrix(KV_W)
    ncp = S // CMP_STRIDE
    ncs = past // CMP_STRIDE
    nb_p = S // SEL_BLOCK
    nb_s = -(-(past + DS) // SEL_BLOCK)
    nl_s = _round_up(nb_s, LANES)
    keys_s = _round_up(past + DS, LANES)
    ovt_p = _overlap_matrix(ncp, nb_p, LANES).T
    oh_p = _expand_matrix(LANES, nb_p, S).T
    gx = _gate_expand_matrix()
    ov_s = _overlap_matrix(ncs, nb_s, nl_s)
    ex_s = _expand_matrix(nl_s, nb_s, keys_s)
    pt_flat = page_table.reshape(-1).astype(jnp.int32)
    pt_prompt = jnp.arange(B * (S // PAGE_SIZE), dtype=jnp.int32)

    o_g = NSA_WIDTH + 6 * KV_W
    w_cat = jnp.concatenate([_pair_order(w_in[:, :, :NSA_WIDTH], 2), w_in[:, :, NSA_WIDTH:o_g],
                             jnp.pad(w_in[:, :, o_g:o_g + GATE_COLS], ((0, 0), (0, 0), (0, LANES - GATE_COLS))),
                             w_in[:, :, o_g + GATE_COLS:]], axis=2).astype(BF16)
    w_out_b = jnp.concatenate([_pair_order(w_out[:, :NSA_WIDTH], 1), w_out[:, NSA_WIDTH:]], axis=1).astype(BF16)
    w_up_b = w_up.astype(BF16)
    w_down_b = w_down.astype(BF16)
    conv_w_p = jnp.pad(conv_w, ((0, 0), (0, _round_up(CONV_WIDTH, SUBLANES) - CONV_WIDTH), (0, 0)))
    fcw_p = jnp.pad(ffn_conv_w, ((0, 0), (0, SUBLANES - FFN_CONV_WIDTH), (0, 0)))

    kc_pool, vc_pool = (a.reshape(depth * n_pool, PAGE_CHUNKS, CHUNK_W) for a in (cache_k_cmp, cache_v_cmp))
    ks_pool, vs_pool = (a.reshape(depth * n_pool, PAGE_SIZE, KV_W) for a in (cache_k_slc, cache_v_slc))

    hpc_p = _round_up(CONV_WIDTH - 1, SUBLANES)
    hpf_p = _round_up(FFN_CONV_WIDTH - 1, SUBLANES)
    conv0 = jnp.zeros((B, hpc_p, CONV_CH), F32)
    ffn0 = jnp.zeros((B, hpf_p, D_FF), F32)

    y_p = x_prompt.reshape(B * S, D_MODEL)
    y_s = x_sample.transpose(1, 0, 2).reshape(TS, D_MODEL)

    def to_bm(a):
        return a.reshape(DS, DB, a.shape[-1]).transpose(1, 0, 2)

    def heads(a, lead):
        return a.reshape(lead + (NSA_KV_HEADS, HEAD_DIM))

    outs = [[] for _ in range(16)]
    for l in range(depth):
        qg = _tile_lanes(q_norm[l], NSA_HEADS)
        kg = jnp.concatenate([_tile_lanes(k_norm[l, i], NSA_KV_HEADS) for i in range(N_BRANCH)], axis=1)
        kcg = _tile_lanes(kc_norm[l], NSA_KV_HEADS)
        wab_k, w2_k, pos_k = _compress_weights(cmp_w1_k[l], cmp_w2_k[l], cmp_pos[l])
        wab_v, w2_v, pos_v = _compress_weights(cmp_w1_v[l], cmp_w2_v[l], cmp_pos[l])
        merge_w = (conv_w_p[l], conv_b[l][None], conv_ln_g[l][None], conv_ln_b[l][None], w_out_b[l],
                   ffn_norm[l][None], w_up_b[l], fcw_p[l], ffn_conv_b[l][None], w_down_b[l])
        pt_l = pt_flat + l * n_pool

        q, kc, vc, ks, vs, kw, vw, gt, u = _inproj(y_p, attn_norm[l][None], w_cat[l], qg, kg, cos_p, sin_p, bdq,
                                                   PROMPT_TM)
        kcmp = _compress(pt_prompt, kc.reshape(-1, PAGE_CHUNKS, CHUNK_W), wab_k, pos_k, w2_k, kcg, bdk, B, True)
        vcmp = _compress(pt_prompt, vc.reshape(-1, PAGE_CHUNKS, CHUNK_W), wab_v, pos_v, w2_v, kcg, bdk, B, False)
        o = _attn_prompt(q, gt, kcmp, vcmp, ks, vs, kw, vw, ovt_p, oh_p, gx, B, S)
        y_p, tail = _merge_ffn(y_p, o, u, conv0, ffn0, *merge_w, FFN_TM, 1)
        for idx, a in enumerate((kc, vc, ks, vs)):
            outs[idx].append(heads(a, (B, S)))
        outs[4].append(heads(kw, (B, S))[:, -p_win:])
        outs[5].append(heads(vw, (B, S))[:, -p_win:])
        outs[6].append(u.reshape(B, S, CONV_CH)[:, -(CONV_WIDTH - 1):])
        outs[7].append(tail[:, -(FFN_CONV_WIDTH - 1):])

        q, kc, vc, ks, vs, kw, vw, gt, u = _inproj(y_s, attn_norm[l][None], w_cat[l], qg, kg, cos_s, sin_s, bdq, TS)
        kcmp = _compress(pt_l, kc_pool, wab_k, pos_k, w2_k, kcg, bdk, DB, True)
        vcmp = _compress(pt_l, vc_pool, wab_v, pos_v, w2_v, kcg, bdk, DB, False)
        kc, vc, ks, vs, kw, vw, u_bm = (to_bm(a) for a in (kc, vc, ks, vs, kw, vw, u))
        kws = state_k_win[l].reshape(DB, win_buf, KV_W)
        vws = state_v_win[l].reshape(DB, win_buf, KV_W)
        o = _attn_sample(pt_l, to_bm(q), to_bm(gt), kcmp, vcmp, ks, vs, kws, vws, kw, vw, ov_s, ex_s,
                         ks_pool, vs_pool, past)
        o_tm = o.transpose(1, 0, 2).reshape(TS, NSA_WIDTH)
        cbuf = state_conv[l].transpose(1, 0, 2).reshape(1, (CONV_WIDTH - 1) * DB, CONV_CH)
        fbuf = state_ffn_conv[l].transpose(1, 0, 2).reshape(1, (FFN_CONV_WIDTH - 1) * DB, D_FF)
        y_s, tail = _merge_ffn(y_s, o_tm, u, cbuf, fbuf, *merge_w, TS, DB)
        for idx, a in enumerate((kc, vc, ks, vs)):
            outs[8 + idx].append(heads(a, (DB, DS)))
        outs[12].append(heads(jnp.concatenate([kws, kw], 1)[:, -win_buf:], (DB, win_buf)))
        outs[13].append(heads(jnp.concatenate([vws, vw], 1)[:, -win_buf:], (DB, win_buf)))
        outs[14].append(jnp.concatenate([state_conv[l], u_bm], 1)[:, -(CONV_WIDTH - 1):])
        outs[15].append(tail.reshape(FFN_CONV_WIDTH - 1, DB, D_FF).transpose(1, 0, 2))

    y_prompt = y_p.reshape(B, S, D_MODEL)
    y_sample = y_s.reshape(DS, DB, D_MODEL).transpose(1, 0, 2)
    return (y_prompt, y_sample) + tuple(jnp.stack(o) for o in outs)
```

```python
import functools

import jax
import jax.numpy as jnp
from jax import lax
from jax.experimental import pallas as pl
from jax.experimental.pallas import tpu as pltpu

F32 = jnp.float32
BF16 = jnp.bfloat16

D_MODEL = 1024
HEAD_DIM = 64
NSA_HEADS = 8
NSA_KV_HEADS = 2
NSA_GROUP = NSA_HEADS // NSA_KV_HEADS
NSA_WIDTH = NSA_HEADS * HEAD_DIM
KV_W = NSA_KV_HEADS * HEAD_DIM
CONV_CH = D_MODEL - NSA_WIDTH
N_BRANCH = 3
CMP_BLOCK = 32
CMP_STRIDE = 16
CMP_HIDDEN = 128
SEL_BLOCK = 64
N_SELECT = 16
WINDOW = 512
CONV_WIDTH = 31
D_FF = 2816
FFN_CONV_WIDTH = 3
ROPE_THETA = 10000.0
EPS = 1e-6
Q_BLOCK = 128
GATE_COLS = NSA_HEADS * N_BRANCH
PAGE_SIZE = 128
SCALE = HEAD_DIM ** -0.5
NEG = -1e30

LANES = 128
SUBLANES = 8
VMEM_LIMIT = 52 * 1024 * 1024
PAGE_CHUNKS = PAGE_SIZE // CMP_STRIDE
PAGE_ROWS = PAGE_SIZE * NSA_KV_HEADS
CHUNK_ROWS = CMP_STRIDE * NSA_KV_HEADS
Z_COLS = NSA_WIDTH + 6 * KV_W + LANES + 2 * CONV_CH

assert NSA_KV_HEADS == 2 and KV_W == LANES and CMP_BLOCK == 2 * CMP_STRIDE


def _round_up(x, m):
    return (x + m - 1) // m * m


def _bdot(a, b):
    return jnp.dot(a.astype(BF16), b.astype(BF16), preferred_element_type=F32)


def _bdot_nt(a, b):
    return lax.dot_general(a.astype(BF16), b.astype(BF16), (((1,), (1,)), ((), ())),
                           preferred_element_type=F32)


def _split(a):
    hi = a.astype(BF16)
    return hi, (a - hi.astype(F32)).astype(BF16)


def _split_dot(a, b_bf16):
    hi, lo = _split(a)
    return (jnp.dot(hi, b_bf16, preferred_element_type=F32)
            + jnp.dot(lo, b_bf16, preferred_element_type=F32))


def _head_rms(x, bd, gain):
    ms = _split_dot(x * x, bd) * (1.0 / HEAD_DIM)
    return x * lax.rsqrt(ms + EPS) * gain


def _rope(y, cos, sin_signed):
    w = y.shape[-1]
    half = HEAD_DIM // 2
    fwd = pltpu.roll(y, w - half, 1)
    bwd = pltpu.roll(y, half, 1)
    lane = lax.broadcasted_iota(jnp.int32, y.shape, 1)
    swapped = jnp.where((lane & (HEAD_DIM - 1)) < half, fwd, bwd)
    return y * cos + swapped * sin_signed


def _masked_softmax(s, mask):
    s = jnp.where(mask, s, NEG)
    m = jnp.max(s, -1, keepdims=True)
    e = jnp.where(mask, jnp.exp(s - m), 0.0)
    return e / jnp.maximum(jnp.sum(e, -1, keepdims=True), 1e-30)


def _softmax_start(rows, width):
    return jnp.full((rows, 1), NEG, F32), jnp.zeros((rows, 1), F32), jnp.zeros((rows, width), F32)


def _softmax_update(state, s, mask, v):
    m_i, l_i, acc = state
    s = jnp.where(mask, s, NEG)
    m_new = jnp.maximum(m_i, jnp.max(s, -1, keepdims=True))
    alpha = jnp.exp(m_i - m_new)
    e = jnp.where(mask, jnp.exp(s - m_new), 0.0)
    return m_new, alpha * l_i + jnp.sum(e, -1, keepdims=True), alpha * acc + _bdot(e, v)


def _softmax_finish(state):
    _, l_i, acc = state
    return acc / jnp.maximum(l_i, 1e-30)


def _const_spec(shape):
    nd = len(shape)
    return pl.BlockSpec(shape, lambda *_: (0,) * nd, pipeline_mode=pl.Buffered(1))


def _inproj_kernel(x_ref, g_ref, w_ref, qg_ref, kg_ref, cos_ref, sin_ref, bd_ref,
                   q_ref, kc_ref, vc_ref, ks_ref, vs_ref, kw_ref, vw_ref, gt_ref, u_ref):
    x = x_ref[...]
    h = x * lax.rsqrt(jnp.mean(x * x, -1, keepdims=True) + EPS) * g_ref[...]
    z = jnp.dot(h.astype(BF16), w_ref[...], preferred_element_type=F32)
    cos = cos_ref[...]
    sin = sin_ref[...]
    bdq = bd_ref[...]
    bdk = bdq[:KV_W, :KV_W]
    nrep = NSA_WIDTH // KV_W
    q = _head_rms(z[:, :NSA_WIDTH], bdq, qg_ref[...])
    q_ref[...] = _rope(q, jnp.concatenate([cos] * nrep, axis=1), jnp.concatenate([sin] * nrep, axis=1))
    kg = kg_ref[...]
    o = NSA_WIDTH
    kc_ref[...] = _rope(_head_rms(z[:, o:o + KV_W], bdk, kg[:, 0:KV_W]), cos, sin)
    vc_ref[...] = z[:, o + KV_W:o + 2 * KV_W]
    ks_ref[...] = _rope(_head_rms(z[:, o + 2 * KV_W:o + 3 * KV_W], bdk, kg[:, KV_W:2 * KV_W]), cos, sin)
    vs_ref[...] = z[:, o + 3 * KV_W:o + 4 * KV_W]
    kw_ref[...] = _rope(_head_rms(z[:, o + 4 * KV_W:o + 5 * KV_W], bdk, kg[:, 2 * KV_W:3 * KV_W]), cos, sin)
    vw_ref[...] = z[:, o + 5 * KV_W:o + 6 * KV_W]
    o += 6 * KV_W
    gt_ref[...] = jax.nn.sigmoid(z[:, o:o + LANES])
    o += LANES
    u_ref[...] = z[:, o:o + CONV_CH] * jax.nn.sigmoid(z[:, o + CONV_CH:o + 2 * CONV_CH])


def _inproj(x, g, w, qg, kg, cos, sin, bd, tm):
    t = x.shape[0]
    nt = t // tm
    npos = cos.shape[0] // tm
    row = lambda i: (i, 0)
    pos = lambda i: (i % npos, 0)
    widths = [NSA_WIDTH] + [KV_W] * 6 + [LANES, CONV_CH]
    return pl.pallas_call(
        _inproj_kernel,
        grid=(nt,),
        in_specs=[pl.BlockSpec((tm, D_MODEL), row), _const_spec((1, D_MODEL)), _const_spec((D_MODEL, Z_COLS)),
                  _const_spec((1, NSA_WIDTH)), _const_spec((1, 3 * KV_W)),
                  pl.BlockSpec((tm, KV_W), pos), pl.BlockSpec((tm, KV_W), pos),
                  _const_spec((NSA_WIDTH, NSA_WIDTH))],
        out_specs=[pl.BlockSpec((tm, wd), row) for wd in widths],
        out_shape=[jax.ShapeDtypeStruct((t, wd), F32) for wd in widths],
        compiler_params=pltpu.CompilerParams(dimension_semantics=("arbitrary",), vmem_limit_bytes=VMEM_LIMIT),
        name="inproj",
    )(x, g, w, qg, kg, cos, sin, bd)


def _page_copies(pt_ref, pool_ref, dst, sem, first, n_pages):
    copies = []
    for p in range(n_pages):
        row0 = pl.multiple_of(pt_ref[first + p] * PAGE_ROWS, PAGE_ROWS)
        copies.append(pltpu.make_async_copy(pool_ref.at[pl.ds(row0, PAGE_ROWS)],
                                            dst.at[pl.ds(p * PAGE_ROWS, PAGE_ROWS)], sem))
    return copies


def _compress_kernel(pt_ref, pool_ref, wt_ref, w1_ref, pos_ref, w2_ref, g_ref, out_ref, buf, sem,
                     *, n_pages, norm):
    b = pl.program_id(0)
    nb = pl.num_programs(0)

    def copies(batch, slot):
        return _page_copies(pt_ref, pool_ref, buf.at[slot], sem.at[slot], batch * n_pages, n_pages)

    @pl.when(b == 0)
    def _():
        for c in copies(0, 0):
            c.start()

    @pl.when(b + 1 < nb)
    def _():
        for c in copies(b + 1, (b + 1) % 2):
            c.start()

    slot = b % 2
    for c in copies(b, slot):
        c.wait()

    nch = n_pages * PAGE_CHUNKS
    acc = jnp.zeros((NSA_KV_HEADS * nch, 2 * CMP_HIDDEN), F32)
    for t in range(CMP_STRIDE):
        x = jnp.concatenate([buf[slot, pl.ds(NSA_KV_HEADS * t + h, nch, stride=CHUNK_ROWS), :]
                             for h in range(NSA_KV_HEADS)], axis=0)
        acc = acc + jnp.dot(x.astype(BF16), wt_ref[t], preferred_element_type=F32)
    bias = jnp.dot(pos_ref[...].astype(BF16), w1_ref[...], preferred_element_type=F32)[0:1]
    nxt = pltpu.roll(acc[:, CMP_HIDDEN:], NSA_KV_HEADS * nch - 1, 0)
    hid = jax.nn.gelu(acc[:, :CMP_HIDDEN] + nxt + bias)
    y = jnp.dot(hid.astype(BF16), w2_ref[...], preferred_element_type=F32)
    if norm:
        y = y * lax.rsqrt(jnp.mean(y * y, -1, keepdims=True) + EPS) * g_ref[...]
    rowi = lax.broadcasted_iota(jnp.int32, y.shape, 0)
    last = (rowi == nch - 1) | (rowi == NSA_KV_HEADS * nch - 1)
    out_ref[0] = jnp.where(last, 0.0, y)


def _compress(page_table, pool, wt, w1, pos, w2, g, n_batch, norm):
    n_pages = page_table.shape[0] // n_batch
    nch = n_pages * PAGE_CHUNKS
    full = lambda a: pl.BlockSpec(a.shape, lambda b, pt: (0,) * a.ndim)
    grid_spec = pltpu.PrefetchScalarGridSpec(
        num_scalar_prefetch=1,
        grid=(n_batch,),
        in_specs=[pl.BlockSpec(memory_space=pl.ANY), full(wt), full(w1), full(pos), full(w2), full(g)],
        out_specs=pl.BlockSpec((1, NSA_KV_HEADS * nch, HEAD_DIM), lambda b, pt: (b, 0, 0)),
        scratch_shapes=[pltpu.VMEM((2, n_pages * PAGE_ROWS, HEAD_DIM), F32), pltpu.SemaphoreType.DMA((2,))],
    )
    return pl.pallas_call(
        functools.partial(_compress_kernel, n_pages=n_pages, norm=norm),
        grid_spec=grid_spec,
        out_shape=jax.ShapeDtypeStruct((n_batch, NSA_KV_HEADS * nch, HEAD_DIM), F32),
        compiler_params=pltpu.CompilerParams(dimension_semantics=("arbitrary",), vmem_limit_bytes=VMEM_LIMIT),
        name="compress",
    )(page_table, pool, wt, w1, pos, w2, g)


def _select_lanes(imp, cur, n_blocks):
    n = lax.broadcasted_iota(jnp.int32, (1, imp.shape[1]), 1)
    forced = (n == 0) | (n == cur) | (n == cur - 1)
    v = jnp.where(forced, jnp.inf, jnp.where(n <= cur, imp, -jnp.inf))
    rank = jnp.zeros(imp.shape, jnp.int32)
    for m in range(n_blocks):
        vm = v[:, m:m + 1]
        before = (n > m).astype(jnp.int32)
        rank = rank + jnp.where(vm > v, 1, jnp.where(vm == v, before, 0))
    return jnp.where(rank < min(N_SELECT, n_blocks), 1.0, 0.0).astype(BF16)


def _select_rows(imp_t, cur, n_blocks):
    n = lax.broadcasted_iota(jnp.int32, imp_t.shape, 0)
    forced = (n == 0) | (n == cur) | (n == cur - 1)
    v = jnp.where(forced, jnp.inf, jnp.where(n <= cur, imp_t, -jnp.inf))
    rank = jnp.zeros(imp_t.shape, jnp.int32)
    for m in range(n_blocks):
        vm = v[m:m + 1, :]
        rank = rank + jnp.where(vm > v, 1, jnp.where(vm == v, (n > m).astype(jnp.int32), 0))
    return jnp.where(rank < min(N_SELECT, n_blocks), 1.0, 0.0)


KEY_BLOCK = 512


def _attn_prompt_kernel(q_ref, gt_ref, kcmp_ref, vcmp_ref, ks_ref, vs_ref, kw_ref, vw_ref, ovt_ref, oh_ref, gx_ref,
                        o_ref, ksa, vsb, kwb, vwb, kcb, vcb, *, seq):
    i = pl.program_id(1)
    n_cmp = kcb.shape[0]

    @pl.when(i == 0)
    def _():
        ksa[:, 0:KV_W] = ks_ref[0].astype(BF16)
        ksa[:, KV_W:2 * KV_W] = oh_ref[...]
        vsb[...] = vs_ref[0].astype(BF16)
        kwb[...] = kw_ref[0].astype(BF16)
        vwb[...] = vw_ref[0].astype(BF16)
        kc = kcmp_ref[0]
        vc = vcmp_ref[0]
        kcb[...] = jnp.concatenate([kc[0:n_cmp], kc[n_cmp:2 * n_cmp]], axis=1).astype(BF16)
        vcb[...] = jnp.concatenate([vc[0:n_cmp], vc[n_cmp:2 * n_cmp]], axis=1).astype(BF16)

    qs = i * Q_BLOCK
    q = q_ref[...] * SCALE
    rows = NSA_GROUP * Q_BLOCK
    lane = lax.broadcasted_iota(jnp.int32, (1, KV_W), 1)
    qpos4 = qs + (lax.broadcasted_iota(jnp.int32, (rows, 1), 0) & (Q_BLOCK - 1))
    qpos_l = qs + lax.broadcasted_iota(jnp.int32, (1, Q_BLOCK), 1)
    n_blocks = seq // SEL_BLOCK
    win_keys = WINDOW + Q_BLOCK
    n_kb = (qs + Q_BLOCK + KEY_BLOCK - 1) // KEY_BLOCK
    w0 = pl.multiple_of(jnp.maximum(qs - WINDOW, 0), Q_BLOCK)
    nt_dims = (((1,), (1,)), ((), ()))
    o_c, o_s, o_w = [], [], []
    for h in range(NSA_KV_HEADS):
        in_head = (lane >= h * HEAD_DIM) & (lane < (h + 1) * HEAD_DIM)
        q4 = jnp.concatenate([jnp.where(in_head, q[:, g * KV_W:(g + 1) * KV_W], 0.0) for g in range(NSA_GROUP)],
                             axis=0).astype(BF16)
        s_c = lax.dot_general(q4, kcb[...], nt_dims, preferred_element_type=F32)
        c = lax.broadcasted_iota(jnp.int32, (1, n_cmp), 1)
        p_c = _masked_softmax(s_c, (c * CMP_STRIDE + (CMP_BLOCK - 1)) <= qpos4)
        o_c.append(jnp.dot(p_c.astype(BF16), vcb[...], preferred_element_type=F32))
        p_sum = p_c[0:Q_BLOCK]
        for g in range(1, NSA_GROUP):
            p_sum = p_sum + p_c[g * Q_BLOCK:(g + 1) * Q_BLOCK]
        hi, lo = _split(p_sum)
        ovt = ovt_ref[...]
        imp_t = (lax.dot_general(ovt, hi, nt_dims, preferred_element_type=F32)
                 + lax.dot_general(ovt, lo, nt_dims, preferred_element_type=F32))[0:n_blocks]
        sel_t = _select_rows(imp_t, qpos_l // SEL_BLOCK, n_blocks)
        bias_t = jnp.concatenate([jnp.where(sel_t > 0.5, 0.0, NEG),
                                  jnp.zeros((KV_W - n_blocks, Q_BLOCK), F32)], axis=0)
        bias = bias_t.T.astype(BF16)
        q4a = jnp.concatenate([q4, jnp.concatenate([bias] * NSA_GROUP, axis=0)], axis=1)

        def sel_block(j, carry, causal):
            m_i, l_i, acc = carry
            k0 = pl.multiple_of(j * KEY_BLOCK, KEY_BLOCK)
            s = lax.dot_general(q4a, ksa[pl.ds(k0, KEY_BLOCK), :], nt_dims, preferred_element_type=F32)
            if causal:
                tok = k0 + lax.broadcasted_iota(jnp.int32, (1, KEY_BLOCK), 1)
                s = jnp.where(tok <= qpos4, s, NEG)
            m_new = jnp.maximum(m_i, jnp.max(s, -1, keepdims=True))
            alpha = jnp.exp(m_i - m_new)
            e = jnp.exp(s - m_new)
            l_new = alpha * l_i + jnp.sum(e, -1, keepdims=True)
            acc_new = alpha * acc + jnp.dot(e.astype(BF16), vsb[pl.ds(k0, KEY_BLOCK), :],
                                            preferred_element_type=F32)
            return m_new, l_new, acc_new

        carry = _softmax_start(rows, KV_W)
        carry = lax.fori_loop(0, n_kb - 1, functools.partial(sel_block, causal=False), carry)
        _, l_s, acc_s = sel_block(n_kb - 1, carry, True)
        o_s.append(acc_s * (1.0 / l_s))
        s_w = lax.dot_general(q4, kwb[pl.ds(w0, win_keys), :], nt_dims, preferred_element_type=F32)
        dp = qpos4 - (w0 + lax.broadcasted_iota(jnp.int32, (1, win_keys), 1))
        s_w = jnp.where((dp >= 0) & (dp < WINDOW), s_w, NEG)
        e_w = jnp.exp(s_w - jnp.max(s_w, -1, keepdims=True))
        acc_w = jnp.dot(e_w.astype(BF16), vwb[pl.ds(w0, win_keys), :], preferred_element_type=F32)
        o_w.append(acc_w * (1.0 / jnp.sum(e_w, -1, keepdims=True)))

    first = lane < HEAD_DIM
    ghi, glo = _split(gt_ref[...])
    out = jnp.zeros((Q_BLOCK, NSA_WIDTH), F32)
    for br, parts in enumerate((o_c, o_s, o_w)):
        gx = gx_ref[br]
        gate = jnp.dot(ghi, gx, preferred_element_type=F32) + jnp.dot(glo, gx, preferred_element_type=F32)
        merged = jnp.concatenate([jnp.where(first, parts[0][g * Q_BLOCK:(g + 1) * Q_BLOCK],
                                            parts[1][g * Q_BLOCK:(g + 1) * Q_BLOCK])
                                  for g in range(NSA_GROUP)], axis=1)
        out = out + gate * merged
    o_ref[...] = out


def _attn_prompt(q, gt, kcmp, vcmp, ks, vs, kw, vw, ovt, oh, gx, n_batch, seq):
    nq = seq // Q_BLOCK
    n_cmp = kcmp.shape[1] // NSA_KV_HEADS
    tok = lambda b, i: (b * nq + i, 0)
    per_b = lambda b, i: (b, 0, 0)
    kv3 = lambda a: a.reshape(n_batch, seq, KV_W)
    cmp_spec = pl.BlockSpec((1, NSA_KV_HEADS * n_cmp, HEAD_DIM), per_b)
    return pl.pallas_call(
        functools.partial(_attn_prompt_kernel, seq=seq),
        grid=(n_batch, nq),
        in_specs=[pl.BlockSpec((Q_BLOCK, NSA_WIDTH), tok), pl.BlockSpec((Q_BLOCK, LANES), tok),
                  cmp_spec, cmp_spec,
                  pl.BlockSpec((1, seq, KV_W), per_b), pl.BlockSpec((1, seq, KV_W), per_b),
                  pl.BlockSpec((1, seq, KV_W), per_b), pl.BlockSpec((1, seq, KV_W), per_b),
                  _const_spec(ovt.shape), _const_spec(oh.shape), _const_spec(gx.shape)],
        out_specs=pl.BlockSpec((Q_BLOCK, NSA_WIDTH), tok),
        out_shape=jax.ShapeDtypeStruct((n_batch * seq, NSA_WIDTH), F32),
        scratch_shapes=[pltpu.VMEM((seq, 2 * KV_W), BF16), pltpu.VMEM((seq, KV_W), BF16),
                        pltpu.VMEM((seq, KV_W), BF16), pltpu.VMEM((seq, KV_W), BF16),
                        pltpu.VMEM((n_cmp, KV_W), BF16), pltpu.VMEM((n_cmp, KV_W), BF16)],
        compiler_params=pltpu.CompilerParams(dimension_semantics=("arbitrary", "arbitrary"),
                                             vmem_limit_bytes=VMEM_LIMIT),
        name="attn_prompt",
    )(q, gt, kcmp, vcmp, kv3(ks), kv3(vs), kv3(kw), kv3(vw), ovt, oh, gx)


SAMPLE_CHUNK_PAGES = 16


def _attn_sample_kernel(pt_ref, q_ref, gt_ref, kcmp_ref, vcmp_ref, ksn_ref, vsn_ref, kws_ref, vws_ref,
                        kwn_ref, vwn_ref, ov_ref, ex_ref, kpool_ref, vpool_ref,
                        o_ref, kbuf, vbuf, ksem, vsem, *, n_pages, past):
    b = pl.program_id(0)
    nb = pl.num_programs(0)
    n_chunks = n_pages // SAMPLE_CHUNK_PAGES
    chunk_tokens = SAMPLE_CHUNK_PAGES * PAGE_SIZE
    chunk_blocks = chunk_tokens // SEL_BLOCK
    assert n_chunks % 2 == 0

    def copies(batch, chunk):
        slot = chunk % 2
        first = batch * n_pages + chunk * SAMPLE_CHUNK_PAGES
        return (_page_copies(pt_ref, kpool_ref, kbuf.at[slot], ksem.at[slot], first, SAMPLE_CHUNK_PAGES)
                + _page_copies(pt_ref, vpool_ref, vbuf.at[slot], vsem.at[slot], first, SAMPLE_CHUNK_PAGES))

    @pl.when(b == 0)
    def _():
        for c in copies(0, 0):
            c.start()

    ds = q_ref.shape[1]
    q = q_ref[0] * SCALE
    gt = gt_ref[0]
    per_head = NSA_GROUP * ds
    qp = jnp.concatenate([q[:, g * KV_W + h * HEAD_DIM:g * KV_W + (h + 1) * HEAD_DIM]
                          for h in range(NSA_KV_HEADS) for g in range(NSA_GROUP)], axis=0).astype(BF16)
    rows = qp.shape[0]
    rowi = lax.broadcasted_iota(jnp.int32, (rows, 1), 0)
    hrow = rowi // per_head
    trow = rowi & (ds - 1)
    qpos = past + trow

    def head_rows(width):
        col = lax.broadcasted_iota(jnp.int32, (1, width), 1)
        return (col & (NSA_KV_HEADS - 1)) == hrow, col // NSA_KV_HEADS

    n_cmp = kcmp_ref.shape[1] // NSA_KV_HEADS
    s_c = _bdot_nt(qp, kcmp_ref[0])
    col = lax.broadcasted_iota(jnp.int32, (1, NSA_KV_HEADS * n_cmp), 1)
    mask_c = ((col // n_cmp) == hrow) & (((col & (n_cmp - 1)) * CMP_STRIDE + (CMP_BLOCK - 1)) <= qpos)
    p_c = _masked_softmax(s_c, mask_c)
    o_c = _bdot(p_c, vcmp_ref[0])
    sums = []
    for h in range(NSA_KV_HEADS):
        acc = p_c[h * per_head:h * per_head + ds]
        for g in range(1, NSA_GROUP):
            acc = acc + p_c[h * per_head + g * ds:h * per_head + (g + 1) * ds]
        sums.append(acc)
    imp = _split_dot(jnp.concatenate(sums, axis=0), ov_ref[...])
    qpos_sel = past + (lax.broadcasted_iota(jnp.int32, (NSA_KV_HEADS * ds, 1), 0) & (ds - 1))
    n_blocks = -(-(past + ds) // SEL_BLOCK)
    sel = _select_lanes(imp, qpos_sel // SEL_BLOCK, n_blocks)
    sel = jnp.concatenate([sel[h * ds:(h + 1) * ds] for h in range(NSA_KV_HEADS) for _ in range(NSA_GROUP)],
                          axis=0)

    state = _softmax_start(rows, HEAD_DIM)
    for c in range(n_chunks):
        if c + 1 < n_chunks:
            for cp in copies(b, c + 1):
                cp.start()
        else:
            @pl.when(b + 1 < nb)
            def _():
                for cp in copies(b + 1, 0):
                    cp.start()
        for cp in copies(b, c):
            cp.wait()
        slot = c % 2
        parity, _ = head_rows(NSA_KV_HEADS * chunk_tokens)
        selm = jnp.dot(sel[:, c * chunk_blocks:(c + 1) * chunk_blocks], ex_ref[...], preferred_element_type=F32)
        state = _softmax_update(state, _bdot_nt(qp, kbuf[slot]), parity & (selm > 0.5), vbuf[slot])
    new_block = past // SEL_BLOCK
    parity, tok = head_rows(NSA_KV_HEADS * ds)
    picked = sel[:, new_block:new_block + 1].astype(F32) > 0.5
    state = _softmax_update(state, _bdot_nt(qp, ksn_ref[0]), parity & picked & (tok <= trow), vsn_ref[0])
    o_s = _softmax_finish(state)

    nwin = kws_ref.shape[1] // NSA_KV_HEADS
    state = _softmax_start(rows, HEAD_DIM)
    parity, tok = head_rows(NSA_KV_HEADS * nwin)
    dp = qpos - (past - nwin + tok)
    state = _softmax_update(state, _bdot_nt(qp, kws_ref[0]), parity & (dp >= 0) & (dp < WINDOW), vws_ref[0])
    parity, tok = head_rows(NSA_KV_HEADS * ds)
    dp = trow - tok
    state = _softmax_update(state, _bdot_nt(qp, kwn_ref[0]), parity & (dp >= 0) & (dp < WINDOW), vwn_ref[0])
    o_w = _softmax_finish(state)

    pieces = []
    for g in range(NSA_GROUP):
        for h in range(NSA_KV_HEADS):
            j = (h * NSA_GROUP + g) * N_BRANCH
            r = slice(h * per_head + g * ds, h * per_head + (g + 1) * ds)
            pieces.append(gt[:, j:j + 1] * o_c[r] + gt[:, j + 1:j + 2] * o_s[r] + gt[:, j + 2:j + 3] * o_w[r])
    o_ref[0] = jnp.concatenate(pieces, axis=1)


def _attn_sample(page_table, q, gt, kcmp, vcmp, ksn, vsn, kws, vws, win_first, kwn, vwn, ov, ex, kpool, vpool, past):
    n_batch, ds = q.shape[:2]
    n_pages = page_table.shape[0] // n_batch
    chunk_rows = SAMPLE_CHUNK_PAGES * PAGE_ROWS
    per_b = lambda b, pt: (b, 0, 0)
    blk = lambda a: pl.BlockSpec((1,) + a.shape[1:], per_b)
    win = lambda a: pl.BlockSpec((1,) + a.shape[1:], lambda b, pt: (win_first + b, 0, 0))
    full = lambda a: pl.BlockSpec(a.shape, lambda b, pt: (0,) * a.ndim)
    grid_spec = pltpu.PrefetchScalarGridSpec(
        num_scalar_prefetch=1,
        grid=(n_batch,),
        in_specs=[blk(q), blk(gt), blk(kcmp), blk(vcmp), blk(ksn), blk(vsn), win(kws), win(vws), blk(kwn), blk(vwn),
                  full(ov), full(ex),
                  pl.BlockSpec(memory_space=pl.ANY), pl.BlockSpec(memory_space=pl.ANY)],
        out_specs=pl.BlockSpec((1, ds, NSA_WIDTH), per_b),
        scratch_shapes=[pltpu.VMEM((2, chunk_rows, HEAD_DIM), F32), pltpu.VMEM((2, chunk_rows, HEAD_DIM), F32),
                        pltpu.SemaphoreType.DMA((2,)), pltpu.SemaphoreType.DMA((2,))],
    )
    return pl.pallas_call(
        functools.partial(_attn_sample_kernel, n_pages=n_pages, past=past),
        grid_spec=grid_spec,
        out_shape=jax.ShapeDtypeStruct((n_batch, ds, NSA_WIDTH), F32),
        compiler_params=pltpu.CompilerParams(dimension_semantics=("arbitrary",), vmem_limit_bytes=VMEM_LIMIT),
        name="attn_sample",
    )(page_table, q, gt, kcmp, vcmp, ksn, vsn, kws, vws, kwn, vwn, ov, ex, kpool, vpool)


CONV_ROWS = 64
FF_SPLIT = 2


def _conv_shifts(hpc, rs):
    off = hpc - (CONV_WIDTH - 1) * rs
    return sorted({(off + k * rs) % SUBLANES for k in range(CONV_WIDTH)} - {0})


def _merge_ffn_kernel(x_ref, o_ref, u_ref, cbuf_ref, fbuf_ref, cw_ref, cb_ref, lng_ref, lnb_ref, wout_ref,
                      fg_ref, wup_ref, fcw_ref, fcb_ref, wdn_ref,
                      y_ref, tail_ref, xx, xsh, cc, gg, *, rs):
    t = pl.program_id(1)
    tm = x_ref.shape[0]
    hpc = xx.shape[0] - tm
    hpf = gg.shape[0] - tm
    off_c = hpc - (CONV_WIDTH - 1) * rs
    off_f = hpf - (FFN_CONV_WIDTH - 1) * rs

    @pl.when(t == 0)
    def _():
        xx[pl.ds(0, hpc), :] = cbuf_ref[0]
        gg[pl.ds(0, hpf), :] = fbuf_ref[0]

    xx[pl.ds(hpc, tm), :] = u_ref[...]
    shifts = _conv_shifts(hpc, rs)
    span = hpc + tm - SUBLANES
    for idx, r in enumerate(shifts):
        xsh[idx, pl.ds(0, span), :] = xx[pl.ds(r, span), :]
    rc = min(CONV_ROWS, tm)
    for c0 in range(0, CONV_CH, LANES):
        for r0 in range(0, tm, rc):
            acc = jnp.zeros((rc, LANES), F32)
            for k in range(CONV_WIDTH):
                start = off_c + k * rs
                r = start % SUBLANES
                if r == 0:
                    tap = xx[pl.ds(start + r0, rc), c0:c0 + LANES]
                else:
                    tap = xsh[shifts.index(r), pl.ds(start - r + r0, rc), c0:c0 + LANES]
                acc = acc + cw_ref[k:k + 1, c0:c0 + LANES] * tap
            cc[pl.ds(r0, rc), c0:c0 + LANES] = acc
    y = cc[...] + cb_ref[...]
    mu = jnp.mean(y, -1, keepdims=True)
    var = jnp.mean(jnp.square(y - mu), -1, keepdims=True)
    y = (y - mu) * lax.rsqrt(var + EPS) * lng_ref[...] + lnb_ref[...]
    c = y * jax.nn.sigmoid(y)
    x1 = (x_ref[...] + _bdot(o_ref[...], wout_ref[0:NSA_WIDTH, :])
          + _bdot(c, wout_ref[NSA_WIDTH:NSA_WIDTH + CONV_CH, :]))
    h = (x1 * lax.rsqrt(jnp.mean(x1 * x1, -1, keepdims=True) + EPS) * fg_ref[...]).astype(BF16)
    fw = D_FF // FF_SPLIT
    f = jnp.zeros((tm, D_MODEL), F32)
    for s in range(FF_SPLIT):
        c0 = s * fw
        gg[pl.ds(hpf, tm), c0:c0 + fw] = jnp.dot(h, wup_ref[:, c0:c0 + fw], preferred_element_type=F32)
        up = jnp.dot(h, wup_ref[:, D_FF + c0:D_FF + c0 + fw], preferred_element_type=F32)
        gc = fcb_ref[:, c0:c0 + fw]
        for k in range(FFN_CONV_WIDTH):
            gc = gc + fcw_ref[k:k + 1, c0:c0 + fw] * gg[pl.ds(off_f + k * rs, tm), c0:c0 + fw]
        act = gc * jax.nn.sigmoid(gc) * up
        f = f + jnp.dot(act.astype(BF16), wdn_ref[c0:c0 + fw, :], preferred_element_type=F32)
    y_ref[...] = x1 + f
    tail_ref[0] = gg[pl.ds(tm, hpf), :]
    xx[pl.ds(0, hpc), :] = xx[pl.ds(tm, hpc), :]
    gg[pl.ds(0, hpf), :] = gg[pl.ds(tm, hpf), :]


def _merge_ffn(x, o, u, cbuf, fbuf, cw, cb, lng, lnb, wout, fg, wup, fcw, fcb, wdn, tm, rs):
    t = x.shape[0]
    n_batch = cbuf.shape[0]
    nt = t // n_batch // tm
    hpc = cbuf.shape[1]
    hpf = fbuf.shape[1]
    n_shift = len(_conv_shifts(hpc, rs))
    xsh_shape = (n_shift, hpc + tm - SUBLANES, CONV_CH) if n_shift else (1, SUBLANES, LANES)
    tok = lambda b, i: (b * nt + i, 0)
    per_b = lambda b, i: (b, 0, 0)
    return pl.pallas_call(
        functools.partial(_merge_ffn_kernel, rs=rs),
        grid=(n_batch, nt),
        in_specs=[pl.BlockSpec((tm, D_MODEL), tok), pl.BlockSpec((tm, NSA_WIDTH), tok),
                  pl.BlockSpec((tm, CONV_CH), tok),
                  pl.BlockSpec((1, hpc, CONV_CH), per_b), pl.BlockSpec((1, hpf, D_FF), per_b),
                  _const_spec(cw.shape), _const_spec(cb.shape), _const_spec(lng.shape), _const_spec(lnb.shape),
                  _const_spec(wout.shape), _const_spec(fg.shape), _const_spec(wup.shape),
                  _const_spec(fcw.shape), _const_spec(fcb.shape), _const_spec(wdn.shape)],
        out_specs=[pl.BlockSpec((tm, D_MODEL), tok), pl.BlockSpec((1, hpf, D_FF), per_b)],
        out_shape=[jax.ShapeDtypeStruct((t, D_MODEL), F32), jax.ShapeDtypeStruct((n_batch, hpf, D_FF), F32)],
        scratch_shapes=[pltpu.VMEM((hpc + tm, CONV_CH), F32), pltpu.VMEM(xsh_shape, F32),
                        pltpu.VMEM((tm, CONV_CH), F32), pltpu.VMEM((hpf + tm, D_FF), F32)],
        compiler_params=pltpu.CompilerParams(dimension_semantics=("arbitrary", "arbitrary"),
                                             vmem_limit_bytes=VMEM_LIMIT),
        name="merge_ffn",
    )(x, o, u, cbuf, fbuf, cw, cb, lng, lnb, wout, fg, wup, fcw, fcb, wdn)


def _rope_tables(pos):
    half = HEAD_DIM // 2
    inv = ROPE_THETA ** (-jnp.arange(half, dtype=F32) / half)
    ang = pos.astype(F32)[:, None] * inv[None, :]
    cos, sin = jnp.cos(ang), jnp.sin(ang)
    reps = KV_W // HEAD_DIM
    return (jnp.tile(jnp.concatenate([cos, cos], -1), (1, reps)),
            jnp.tile(jnp.concatenate([-sin, sin], -1), (1, reps)))


def _head_sum_matrix(width):
    i = jnp.arange(width) // HEAD_DIM
    return (i[:, None] == i[None, :]).astype(BF16)


def _overlap_matrix(n_cmp_rows, n_blocks, n_lanes):
    c0 = jnp.arange(n_cmp_rows) * CMP_STRIDE
    s0 = jnp.arange(n_lanes) * SEL_BLOCK
    ov = jnp.clip(jnp.minimum(c0[:, None] + CMP_BLOCK, s0[None, :] + SEL_BLOCK)
                  - jnp.maximum(c0[:, None], s0[None, :]), 0, CMP_BLOCK).astype(F32) / CMP_BLOCK
    ov = jnp.where(jnp.arange(n_lanes)[None, :] < n_blocks, ov, 0.0)
    return ov.astype(BF16)


def _expand_matrix(n_rows, n_blocks, n_keys, keys_per_block):
    n = jnp.arange(n_rows)[:, None]
    t = jnp.arange(n_keys)[None, :] // keys_per_block
    return ((n == t) & (n < n_blocks)).astype(BF16)


def _gate_expand_matrix():
    col = jnp.arange(NSA_WIDTH)
    head = ((col % KV_W) // HEAD_DIM) * NSA_GROUP + col // KV_W
    j = jnp.arange(LANES)[None, :, None]
    br = jnp.arange(N_BRANCH)[:, None, None]
    return (j == head[None, None, :] * N_BRANCH + br).astype(BF16)


def _pair_order(a, axis):
    shape = a.shape
    a = a.reshape(shape[:axis] + (NSA_KV_HEADS, NSA_GROUP, HEAD_DIM) + shape[axis + 1:])
    return jnp.swapaxes(a, axis, axis + 1).reshape(shape)


def _compress_weights(w1, w2, pos):
    half = CMP_STRIDE * HEAD_DIM
    wt = jnp.concatenate([w1[:half].reshape(CMP_STRIDE, HEAD_DIM, CMP_HIDDEN),
                          w1[half:].reshape(CMP_STRIDE, HEAD_DIM, CMP_HIDDEN)], axis=2).astype(BF16)
    pos8 = jnp.concatenate([pos.reshape(1, CMP_BLOCK * HEAD_DIM),
                            jnp.zeros((SUBLANES - 1, CMP_BLOCK * HEAD_DIM), F32)], axis=0)
    return wt, w1.astype(BF16), pos8, w2.astype(BF16)


def _tile_lanes(v, reps):
    return jnp.tile(v, reps)[None, :]


PROMPT_TM = 512
FFN_TM = 256


def kernel(x_prompt, x_sample, cache_k_cmp, cache_v_cmp, cache_k_slc, cache_v_slc, state_k_win, state_v_win,
           state_conv, state_ffn_conv, page_table, attn_norm, w_in, q_norm, k_norm, kc_norm, cmp_pos,
           cmp_w1_k, cmp_w2_k, cmp_w1_v, cmp_w2_v, conv_w, conv_b, conv_ln_g, conv_ln_b, w_out, ffn_norm,
           w_up, ffn_conv_w, ffn_conv_b, w_down):
    B, S = x_prompt.shape[:2]
    DB, DS = x_sample.shape[:2]
    depth = w_in.shape[0]
    n_pool = cache_k_cmp.shape[1]
    n_pages = page_table.shape[1]
    past = n_pages * PAGE_SIZE
    win_buf = state_k_win.shape[2]
    p_win = min(WINDOW, S)
    TS = DB * DS

    cos_p, sin_p = _rope_tables(jnp.arange(S))
    cos_s, sin_s = _rope_tables(jnp.repeat(past + jnp.arange(DS), DB))
    bdq = _head_sum_matrix(NSA_WIDTH)
    ncp = S // CMP_STRIDE
    ncs = past // CMP_STRIDE
    nb_p = S // SEL_BLOCK
    nb_s = -(-(past + DS) // SEL_BLOCK)
    nl_s = _round_up(nb_s, LANES)
    chunk_blocks = SAMPLE_CHUNK_PAGES * PAGE_SIZE // SEL_BLOCK
    ovt_p = _overlap_matrix(ncp, nb_p, LANES).T
    oh_p = _expand_matrix(LANES, nb_p, S, SEL_BLOCK).T
    gx = _gate_expand_matrix()
    ov_s = jnp.tile(_overlap_matrix(ncs, nb_s, nl_s), (NSA_KV_HEADS, 1))
    ex_s = _expand_matrix(chunk_blocks, chunk_blocks, chunk_blocks * SEL_BLOCK * NSA_KV_HEADS,
                          SEL_BLOCK * NSA_KV_HEADS)
    pt_flat = page_table.reshape(-1).astype(jnp.int32)
    pt_prompt = jnp.arange(B * (S // PAGE_SIZE), dtype=jnp.int32)

    o_g = NSA_WIDTH + 6 * KV_W
    w_cat = jnp.concatenate([_pair_order(w_in[:, :, :NSA_WIDTH], 2), w_in[:, :, NSA_WIDTH:o_g],
                             jnp.pad(w_in[:, :, o_g:o_g + GATE_COLS], ((0, 0), (0, 0), (0, LANES - GATE_COLS))),
                             w_in[:, :, o_g + GATE_COLS:]], axis=2).astype(BF16)
    w_out_b = jnp.concatenate([_pair_order(w_out[:, :NSA_WIDTH], 1), w_out[:, NSA_WIDTH:]], axis=1).astype(BF16)
    w_up_b = w_up.astype(BF16)
    w_down_b = w_down.astype(BF16)
    conv_w_p = jnp.pad(conv_w, ((0, 0), (0, _round_up(CONV_WIDTH, SUBLANES) - CONV_WIDTH), (0, 0)))
    fcw_p = jnp.pad(ffn_conv_w, ((0, 0), (0, SUBLANES - FFN_CONV_WIDTH), (0, 0)))

    kc_pool, vc_pool, ks_pool, vs_pool = (a.reshape(-1, HEAD_DIM)
                                          for a in (cache_k_cmp, cache_v_cmp, cache_k_slc, cache_v_slc))
    kw_states, vw_states = (a.reshape(depth * DB, win_buf * NSA_KV_HEADS, HEAD_DIM)
                            for a in (state_k_win, state_v_win))

    hpc_p = _round_up(CONV_WIDTH - 1, SUBLANES)
    hpf_p = _round_up(FFN_CONV_WIDTH - 1, SUBLANES)
    conv0 = jnp.zeros((B, hpc_p, CONV_CH), F32)
    ffn0 = jnp.zeros((B, hpf_p, D_FF), F32)

    y_p = x_prompt.reshape(B * S, D_MODEL)
    y_s = x_sample.transpose(1, 0, 2).reshape(TS, D_MODEL)

    def to_bm(a):
        return a.reshape(DS, DB, a.shape[-1]).transpose(1, 0, 2)

    def heads(a, lead):
        return a.reshape(lead + (NSA_KV_HEADS, HEAD_DIM))

    def head_rows(a):
        return a.reshape(a.shape[:-3] + (a.shape[-3] * NSA_KV_HEADS, HEAD_DIM))

    outs = [[] for _ in range(16)]
    for l in range(depth):
        qg = _tile_lanes(q_norm[l], NSA_HEADS)
        kg = jnp.concatenate([_tile_lanes(k_norm[l, i], NSA_KV_HEADS) for i in range(N_BRANCH)], axis=1)
        kcg = kc_norm[l][None]
        cw_k = _compress_weights(cmp_w1_k[l], cmp_w2_k[l], cmp_pos[l])
        cw_v = _compress_weights(cmp_w1_v[l], cmp_w2_v[l], cmp_pos[l])
        merge_w = (conv_w_p[l], conv_b[l][None], conv_ln_g[l][None], conv_ln_b[l][None], w_out_b[l],
                   ffn_norm[l][None], w_up_b[l], fcw_p[l], ffn_conv_b[l][None], w_down_b[l])
        pt_l = pt_flat + l * n_pool

        q, kc, vc, ks, vs, kw, vw, gt, u = _inproj(y_p, attn_norm[l][None], w_cat[l], qg, kg, cos_p, sin_p, bdq,
                                                   PROMPT_TM)
        kc4, vc4 = heads(kc, (B, S)), heads(vc, (B, S))
        kcmp = _compress(pt_prompt, kc4.reshape(-1, HEAD_DIM), *cw_k, kcg, B, True)
        vcmp = _compress(pt_prompt, vc4.reshape(-1, HEAD_DIM), *cw_v, kcg, B, False)
        o = _attn_prompt(q, gt, kcmp, vcmp, ks, vs, kw, vw, ovt_p, oh_p, gx, B, S)
        y_p, tail = _merge_ffn(y_p, o, u, conv0, ffn0, *merge_w, FFN_TM, 1)
        outs[0].append(kc4)
        outs[1].append(vc4)
        outs[2].append(heads(ks, (B, S)))
        outs[3].append(heads(vs, (B, S)))
        outs[4].append(heads(kw, (B, S))[:, -p_win:])
        outs[5].append(heads(vw, (B, S))[:, -p_win:])
        outs[6].append(u.reshape(B, S, CONV_CH)[:, -(CONV_WIDTH - 1):])
        outs[7].append(tail[:, -(FFN_CONV_WIDTH - 1):])

        q, kc, vc, ks, vs, kw, vw, gt, u = _inproj(y_s, attn_norm[l][None], w_cat[l], qg, kg, cos_s, sin_s, bdq, TS)
        kcmp = _compress(pt_l, kc_pool, *cw_k, kcg, DB, True)
        vcmp = _compress(pt_l, vc_pool, *cw_v, kcg, DB, False)
        kc, vc, ks, vs, kw, vw = (heads(to_bm(a), (DB, DS)) for a in (kc, vc, ks, vs, kw, vw))
        u_bm = to_bm(u)
        o = _attn_sample(pt_l, to_bm(q), to_bm(gt), kcmp, vcmp, head_rows(ks), head_rows(vs),
                         kw_states, vw_states, l * DB, head_rows(kw), head_rows(vw),
                         ov_s, ex_s, ks_pool, vs_pool, past)
        o_tm = o.transpose(1, 0, 2).reshape(TS, NSA_WIDTH)
        cbuf = state_conv[l].transpose(1, 0, 2).reshape(1, (CONV_WIDTH - 1) * DB, CONV_CH)
        fbuf = state_ffn_conv[l].transpose(1, 0, 2).reshape(1, (FFN_CONV_WIDTH - 1) * DB, D_FF)
        y_s, tail = _merge_ffn(y_s, o_tm, u, cbuf, fbuf, *merge_w, TS, DB)
        for idx, a in enumerate((kc, vc, ks, vs)):
            outs[8 + idx].append(a)
        outs[12].append(jnp.concatenate([state_k_win[l], kw], 1)[:, -win_buf:])
        outs[13].append(jnp.concatenate([state_v_win[l], vw], 1)[:, -win_buf:])
        outs[14].append(jnp.concatenate([state_conv[l], u_bm], 1)[:, -(CONV_WIDTH - 1):])
        outs[15].append(tail.reshape(FFN_CONV_WIDTH - 1, DB, D_FF).transpose(1, 0, 2))

    y_prompt = y_p.reshape(B, S, D_MODEL)
    y_sample = y_s.reshape(DS, DB, D_MODEL).transpose(1, 0, 2)
    return (y_prompt, y_sample) + tuple(jnp.stack(o) for o in outs)
```

```python
import functools

import jax
import jax.numpy as jnp
from jax import lax
from jax.experimental import pallas as pl
from jax.experimental.pallas import tpu as pltpu

F32 = jnp.float32
BF16 = jnp.bfloat16

D_MODEL = 1024
HEAD_DIM = 64
NSA_HEADS = 8
NSA_KV_HEADS = 2
NSA_GROUP = NSA_HEADS // NSA_KV_HEADS
NSA_WIDTH = NSA_HEADS * HEAD_DIM
KV_W = NSA_KV_HEADS * HEAD_DIM
CONV_CH = D_MODEL - NSA_WIDTH
N_BRANCH = 3
CMP_BLOCK = 32
CMP_STRIDE = 16
CMP_HIDDEN = 128
SEL_BLOCK = 64
N_SELECT = 16
WINDOW = 512
CONV_WIDTH = 31
D_FF = 2816
FFN_CONV_WIDTH = 3
ROPE_THETA = 10000.0
EPS = 1e-6
Q_BLOCK = 128
GATE_COLS = NSA_HEADS * N_BRANCH
PAGE_SIZE = 128
SCALE = HEAD_DIM ** -0.5
NEG = -1e30

LANES = 128
SUBLANES = 8
VMEM_LIMIT = 52 * 1024 * 1024
CHUNK_W = CMP_STRIDE * KV_W
PAGE_CHUNKS = PAGE_SIZE // CMP_STRIDE
Z_COLS = NSA_WIDTH + 6 * KV_W + LANES + 2 * CONV_CH
NT_DIMS = (((1,), (1,)), ((), ()))

assert NSA_KV_HEADS == 2 and KV_W == LANES and PAGE_SIZE == LANES and CMP_BLOCK == 2 * CMP_STRIDE


def _round_up(x, m):
    return (x + m - 1) // m * m


def _bdot(a, b):
    return jnp.dot(a.astype(BF16), b.astype(BF16), preferred_element_type=F32)


def _bdot_nt(a, b):
    return lax.dot_general(a.astype(BF16), b.astype(BF16), NT_DIMS, preferred_element_type=F32)


def _split(a):
    hi = a.astype(BF16)
    return hi, (a - hi.astype(F32)).astype(BF16)


def _split_dot(a, b_bf16):
    hi, lo = _split(a)
    return (jnp.dot(hi, b_bf16, preferred_element_type=F32)
            + jnp.dot(lo, b_bf16, preferred_element_type=F32))


def _head_rms(x, bd, gain):
    ms = _split_dot(x * x, bd) * (1.0 / HEAD_DIM)
    return x * lax.rsqrt(ms + EPS) * gain


def _rope(y, cos, sin_signed):
    w = y.shape[-1]
    half = HEAD_DIM // 2
    fwd = pltpu.roll(y, w - half, 1)
    bwd = pltpu.roll(y, half, 1)
    lane = lax.broadcasted_iota(jnp.int32, y.shape, 1)
    swapped = jnp.where((lane & (HEAD_DIM - 1)) < half, fwd, bwd)
    return y * cos + swapped * sin_signed


def _masked_softmax(s, mask):
    s = jnp.where(mask, s, NEG)
    m = jnp.max(s, -1, keepdims=True)
    e = jnp.where(mask, jnp.exp(s - m), 0.0)
    return e / jnp.maximum(jnp.sum(e, -1, keepdims=True), 1e-30)


def _softmax_start(rows, width):
    return jnp.full((rows, 1), NEG, F32), jnp.zeros((rows, 1), F32), jnp.zeros((rows, width), F32)


def _softmax_update(state, s, mask, weigh):
    m_i, l_i, acc = state
    s = jnp.where(mask, s, NEG)
    m_new = jnp.maximum(m_i, jnp.max(s, -1, keepdims=True))
    alpha = jnp.exp(m_i - m_new)
    e = jnp.where(mask, jnp.exp(s - m_new), 0.0)
    return m_new, alpha * l_i + jnp.sum(e, -1, keepdims=True), alpha * acc + weigh(e)


def _softmax_finish(state):
    _, l_i, acc = state
    return acc / jnp.maximum(l_i, 1e-30)


def _const_spec(shape):
    nd = len(shape)
    return pl.BlockSpec(shape, lambda *_: (0,) * nd, pipeline_mode=pl.Buffered(1))


def _inproj_kernel(x_ref, g_ref, w_ref, qg_ref, kg_ref, cos_ref, sin_ref, bd_ref,
                   q_ref, kc_ref, vc_ref, ks_ref, vs_ref, kw_ref, vw_ref, gt_ref, u_ref):
    x = x_ref[...]
    h = x * lax.rsqrt(jnp.mean(x * x, -1, keepdims=True) + EPS) * g_ref[...]
    z = jnp.dot(h.astype(BF16), w_ref[...], preferred_element_type=F32)
    cos = cos_ref[...]
    sin = sin_ref[...]
    bdq = bd_ref[...]
    bdk = bdq[:KV_W, :KV_W]
    nrep = NSA_WIDTH // KV_W
    q = _head_rms(z[:, :NSA_WIDTH], bdq, qg_ref[...])
    q_ref[...] = _rope(q, jnp.concatenate([cos] * nrep, axis=1), jnp.concatenate([sin] * nrep, axis=1))
    kg = kg_ref[...]
    o = NSA_WIDTH
    kc_ref[...] = _rope(_head_rms(z[:, o:o + KV_W], bdk, kg[:, 0:KV_W]), cos, sin)
    vc_ref[...] = z[:, o + KV_W:o + 2 * KV_W]
    ks_ref[...] = _rope(_head_rms(z[:, o + 2 * KV_W:o + 3 * KV_W], bdk, kg[:, KV_W:2 * KV_W]), cos, sin)
    vs_ref[...] = z[:, o + 3 * KV_W:o + 4 * KV_W]
    kw_ref[...] = _rope(_head_rms(z[:, o + 4 * KV_W:o + 5 * KV_W], bdk, kg[:, 2 * KV_W:3 * KV_W]), cos, sin)
    vw_ref[...] = z[:, o + 5 * KV_W:o + 6 * KV_W]
    o += 6 * KV_W
    gt_ref[...] = jax.nn.sigmoid(z[:, o:o + LANES])
    o += LANES
    u_ref[...] = z[:, o:o + CONV_CH] * jax.nn.sigmoid(z[:, o + CONV_CH:o + 2 * CONV_CH])


def _inproj(x, g, w, qg, kg, cos, sin, bd, tm):
    t = x.shape[0]
    nt = t // tm
    npos = cos.shape[0] // tm
    row = lambda i: (i, 0)
    pos = lambda i: (i % npos, 0)
    widths = [NSA_WIDTH] + [KV_W] * 6 + [LANES, CONV_CH]
    return pl.pallas_call(
        _inproj_kernel,
        grid=(nt,),
        in_specs=[pl.BlockSpec((tm, D_MODEL), row), _const_spec((1, D_MODEL)), _const_spec((D_MODEL, Z_COLS)),
                  _const_spec((1, NSA_WIDTH)), _const_spec((1, 3 * KV_W)),
                  pl.BlockSpec((tm, KV_W), pos), pl.BlockSpec((tm, KV_W), pos),
                  _const_spec((NSA_WIDTH, NSA_WIDTH))],
        out_specs=[pl.BlockSpec((tm, wd), row) for wd in widths],
        out_shape=[jax.ShapeDtypeStruct((t, wd), F32) for wd in widths],
        compiler_params=pltpu.CompilerParams(dimension_semantics=("arbitrary",), vmem_limit_bytes=VMEM_LIMIT),
        name="inproj",
    )(x, g, w, qg, kg, cos, sin, bd)


def _compress_rows_kernel(x_ref, wab_ref, pos_ref, w2_ref, g_ref, bd_ref, out_ref, *, norm):
    x = x_ref[0]
    nch = x.shape[0]
    hid2 = NSA_KV_HEADS * CMP_HIDDEN
    wab = wab_ref[...]
    z = jnp.dot(x.astype(BF16), wab, preferred_element_type=F32)
    pos = pos_ref[...].astype(BF16)
    bias = (jnp.dot(pos[0:SUBLANES], wab[:, :hid2], preferred_element_type=F32)
            + jnp.dot(pos[SUBLANES:], wab[:, hid2:], preferred_element_type=F32))[0:1]
    nxt = pltpu.roll(z[:, hid2:], nch - 1, 0)
    hid = jax.nn.gelu(z[:, :hid2] + nxt + bias)
    y = jnp.dot(hid.astype(BF16), w2_ref[...], preferred_element_type=F32)
    if norm:
        y = _head_rms(y, bd_ref[...], g_ref[...])
    rowi = lax.broadcasted_iota(jnp.int32, y.shape, 0)
    out_ref[0] = jnp.where(rowi < nch - 1, y, 0.0)


def _compress_rows(rows, wab, pos, w2, g, bd, n_batch, norm):
    x = rows.reshape(n_batch, -1, CHUNK_W)
    nch = x.shape[1]
    return pl.pallas_call(
        functools.partial(_compress_rows_kernel, norm=norm),
        grid=(n_batch,),
        in_specs=[pl.BlockSpec((1, nch, CHUNK_W), lambda b: (b, 0, 0)), _const_spec(wab.shape),
                  _const_spec(pos.shape), _const_spec(w2.shape), _const_spec(g.shape), _const_spec(bd.shape)],
        out_specs=pl.BlockSpec((1, nch, KV_W), lambda b: (b, 0, 0)),
        out_shape=jax.ShapeDtypeStruct((n_batch, nch, KV_W), F32),
        compiler_params=pltpu.CompilerParams(dimension_semantics=("arbitrary",), vmem_limit_bytes=VMEM_LIMIT),
        name="compress_rows",
    )(x, wab, pos, w2, g, bd)


def _page_copies(pt_ref, pool_ref, dst, sem, first, n_pages):
    copies = []
    for p in range(n_pages):
        page = pt_ref[first + p]
        for h in range(NSA_KV_HEADS):
            copies.append(pltpu.make_async_copy(pool_ref.at[page * NSA_KV_HEADS + h],
                                                dst.at[h, :, pl.ds(p * PAGE_SIZE, PAGE_SIZE)], sem))
    return copies


REGROUP_UNROLL = 8


def _compress_pages_kernel(pt_ref, pool_ref, perm_ref, wt_ref, w1_ref, pos_ref, w2_ref, g_ref, out_ref,
                           buf, xg, sem, *, n_pages, norm):
    b = pl.program_id(0)
    nb = pl.num_programs(0)

    def copies(batch, slot):
        return _page_copies(pt_ref, pool_ref, buf.at[slot], sem.at[slot], batch * n_pages, n_pages)

    @pl.when(b == 0)
    def _():
        for c in copies(0, 0):
            c.start()

    @pl.when(b + 1 < nb)
    def _():
        for c in copies(b + 1, (b + 1) % 2):
            c.start()

    slot = b % 2
    for c in copies(b, slot):
        c.wait()

    perm = perm_ref[...]

    def regroup(i, carry):
        for k in range(REGROUP_UNROLL):
            p = i * REGROUP_UNROLL + k
            t0 = pl.multiple_of(p * PAGE_SIZE, PAGE_SIZE)
            for h in range(NSA_KV_HEADS):
                slab = buf[slot, h, :, pl.ds(t0, PAGE_SIZE)].astype(BF16)
                xg[h, p] = lax.dot_general(perm, slab, NT_DIMS, preferred_element_type=F32)
        return carry

    lax.fori_loop(0, n_pages // REGROUP_UNROLL, regroup, 0)

    nch = n_pages * PAGE_CHUNKS
    acc = jnp.zeros((NSA_KV_HEADS * nch, 2 * CMP_HIDDEN), F32)
    for t in range(CMP_STRIDE):
        x = jnp.concatenate([xg[h, :, pl.ds(t * PAGE_CHUNKS, PAGE_CHUNKS), :].reshape(nch, HEAD_DIM)
                             for h in range(NSA_KV_HEADS)], axis=0)
        acc = acc + jnp.dot(x.astype(BF16), wt_ref[t], preferred_element_type=F32)
    bias = jnp.dot(pos_ref[...].astype(BF16), w1_ref[...], preferred_element_type=F32)[0:1]
    nxt = pltpu.roll(acc[:, CMP_HIDDEN:], NSA_KV_HEADS * nch - 1, 0)
    hid = jax.nn.gelu(acc[:, :CMP_HIDDEN] + nxt + bias)
    y = jnp.dot(hid.astype(BF16), w2_ref[...], preferred_element_type=F32)
    if norm:
        y = y * lax.rsqrt(jnp.mean(y * y, -1, keepdims=True) + EPS) * g_ref[...]
    rowi = lax.broadcasted_iota(jnp.int32, y.shape, 0)
    last = (rowi == nch - 1) | (rowi == NSA_KV_HEADS * nch - 1)
    out_ref[0] = jnp.where(last, 0.0, y)


def _compress_pages(page_table, pool, perm, wt, w1, pos, w2, g, n_batch, norm):
    n_pages = page_table.shape[0] // n_batch
    nch = n_pages * PAGE_CHUNKS
    full = lambda a: pl.BlockSpec(a.shape, lambda b, pt: (0,) * a.ndim)
    grid_spec = pltpu.PrefetchScalarGridSpec(
        num_scalar_prefetch=1,
        grid=(n_batch,),
        in_specs=[pl.BlockSpec(memory_space=pl.ANY), full(perm), full(wt), full(w1), full(pos), full(w2), full(g)],
        out_specs=pl.BlockSpec((1, NSA_KV_HEADS * nch, HEAD_DIM), lambda b, pt: (b, 0, 0)),
        scratch_shapes=[pltpu.VMEM((2, NSA_KV_HEADS, HEAD_DIM, n_pages * PAGE_SIZE), F32),
                        pltpu.VMEM((NSA_KV_HEADS, n_pages, PAGE_SIZE, HEAD_DIM), F32),
                        pltpu.SemaphoreType.DMA((2,))],
    )
    return pl.pallas_call(
        functools.partial(_compress_pages_kernel, n_pages=n_pages, norm=norm),
        grid_spec=grid_spec,
        out_shape=jax.ShapeDtypeStruct((n_batch, NSA_KV_HEADS * nch, HEAD_DIM), F32),
        compiler_params=pltpu.CompilerParams(dimension_semantics=("arbitrary",), vmem_limit_bytes=VMEM_LIMIT),
        name="compress_pages",
    )(page_table, pool, perm, wt, w1, pos, w2, g)


def _select_lanes(imp, cur, n_blocks):
    n = lax.broadcasted_iota(jnp.int32, (1, imp.shape[1]), 1)
    forced = (n == 0) | (n == cur) | (n == cur - 1)
    v = jnp.where(forced, jnp.inf, jnp.where(n <= cur, imp, -jnp.inf))
    rank = jnp.zeros(imp.shape, jnp.int32)
    for m in range(n_blocks):
        vm = v[:, m:m + 1]
        before = (n > m).astype(jnp.int32)
        rank = rank + jnp.where(vm > v, 1, jnp.where(vm == v, before, 0))
    return jnp.where(rank < min(N_SELECT, n_blocks), 1.0, 0.0).astype(BF16)


def _select_rows(imp_t, cur, n_blocks):
    n = lax.broadcasted_iota(jnp.int32, imp_t.shape, 0)
    forced = (n == 0) | (n == cur) | (n == cur - 1)
    v = jnp.where(forced, jnp.inf, jnp.where(n <= cur, imp_t, -jnp.inf))
    rank = jnp.zeros(imp_t.shape, jnp.int32)
    for m in range(n_blocks):
        vm = v[m:m + 1, :]
        rank = rank + jnp.where(vm > v, 1, jnp.where(vm == v, (n > m).astype(jnp.int32), 0))
    return jnp.where(rank < min(N_SELECT, n_blocks), 1.0, 0.0)


KEY_BLOCK = 512


def _attn_prompt_kernel(q_ref, gt_ref, kcmp_ref, vcmp_ref, ks_ref, vs_ref, kw_ref, vw_ref, ovt_ref, oh_ref, gx_ref,
                        o_ref, ksa, vsb, kwb, vwb, kcb, vcb, *, seq):
    i = pl.program_id(1)

    @pl.when(i == 0)
    def _():
        ksa[:, 0:KV_W] = ks_ref[0].astype(BF16)
        ksa[:, KV_W:2 * KV_W] = oh_ref[...]
        vsb[...] = vs_ref[0].astype(BF16)
        kwb[...] = kw_ref[0].astype(BF16)
        vwb[...] = vw_ref[0].astype(BF16)
        kcb[...] = kcmp_ref[0].astype(BF16)
        vcb[...] = vcmp_ref[0].astype(BF16)

    qs = i * Q_BLOCK
    q = q_ref[...] * SCALE
    rows = NSA_GROUP * Q_BLOCK
    lane = lax.broadcasted_iota(jnp.int32, (1, KV_W), 1)
    qpos4 = qs + (lax.broadcasted_iota(jnp.int32, (rows, 1), 0) & (Q_BLOCK - 1))
    qpos_l = qs + lax.broadcasted_iota(jnp.int32, (1, Q_BLOCK), 1)
    n_cmp = kcb.shape[0]
    n_blocks = seq // SEL_BLOCK
    win_keys = WINDOW + Q_BLOCK
    n_kb = (qs + Q_BLOCK + KEY_BLOCK - 1) // KEY_BLOCK
    w0 = pl.multiple_of(jnp.maximum(qs - WINDOW, 0), Q_BLOCK)
    o_c, o_s, o_w = [], [], []
    for h in range(NSA_KV_HEADS):
        in_head = (lane >= h * HEAD_DIM) & (lane < (h + 1) * HEAD_DIM)
        q4 = jnp.concatenate([jnp.where(in_head, q[:, g * KV_W:(g + 1) * KV_W], 0.0) for g in range(NSA_GROUP)],
                             axis=0).astype(BF16)
        s_c = lax.dot_general(q4, kcb[...], NT_DIMS, preferred_element_type=F32)
        c = lax.broadcasted_iota(jnp.int32, (1, n_cmp), 1)
        p_c = _masked_softmax(s_c, (c * CMP_STRIDE + (CMP_BLOCK - 1)) <= qpos4)
        o_c.append(jnp.dot(p_c.astype(BF16), vcb[...], preferred_element_type=F32))
        p_sum = p_c[0:Q_BLOCK]
        for g in range(1, NSA_GROUP):
            p_sum = p_sum + p_c[g * Q_BLOCK:(g + 1) * Q_BLOCK]
        hi, lo = _split(p_sum)
        ovt = ovt_ref[...]
        imp_t = (lax.dot_general(ovt, hi, NT_DIMS, preferred_element_type=F32)
                 + lax.dot_general(ovt, lo, NT_DIMS, preferred_element_type=F32))[0:n_blocks]
        sel_t = _select_rows(imp_t, qpos_l // SEL_BLOCK, n_blocks)
        bias_t = jnp.concatenate([jnp.where(sel_t > 0.5, 0.0, NEG),
                                  jnp.zeros((KV_W - n_blocks, Q_BLOCK), F32)], axis=0)
        bias = bias_t.T.astype(BF16)
        q4a = jnp.concatenate([q4, jnp.concatenate([bias] * NSA_GROUP, axis=0)], axis=1)

        def sel_block(j, carry, causal):
            m_i, l_i, acc = carry
            k0 = pl.multiple_of(j * KEY_BLOCK, KEY_BLOCK)
            s = lax.dot_general(q4a, ksa[pl.ds(k0, KEY_BLOCK), :], NT_DIMS, preferred_element_type=F32)
            if causal:
                tok = k0 + lax.broadcasted_iota(jnp.int32, (1, KEY_BLOCK), 1)
                s = jnp.where(tok <= qpos4, s, NEG)
            m_new = jnp.maximum(m_i, jnp.max(s, -1, keepdims=True))
            alpha = jnp.exp(m_i - m_new)
            e = jnp.exp(s - m_new)
            l_new = alpha * l_i + jnp.sum(e, -1, keepdims=True)
            acc_new = alpha * acc + jnp.dot(e.astype(BF16), vsb[pl.ds(k0, KEY_BLOCK), :],
                                            preferred_element_type=F32)
            return m_new, l_new, acc_new

        carry = _softmax_start(rows, KV_W)
        carry = lax.fori_loop(0, n_kb - 1, functools.partial(sel_block, causal=False), carry)
        _, l_s, acc_s = sel_block(n_kb - 1, carry, True)
        o_s.append(acc_s * (1.0 / l_s))
        s_w = lax.dot_general(q4, kwb[pl.ds(w0, win_keys), :], NT_DIMS, preferred_element_type=F32)
        dp = qpos4 - (w0 + lax.broadcasted_iota(jnp.int32, (1, win_keys), 1))
        s_w = jnp.where((dp >= 0) & (dp < WINDOW), s_w, NEG)
        e_w = jnp.exp(s_w - jnp.max(s_w, -1, keepdims=True))
        acc_w = jnp.dot(e_w.astype(BF16), vwb[pl.ds(w0, win_keys), :], preferred_element_type=F32)
        o_w.append(acc_w * (1.0 / jnp.sum(e_w, -1, keepdims=True)))

    first = lane < HEAD_DIM
    ghi, glo = _split(gt_ref[...])
    out = jnp.zeros((Q_BLOCK, NSA_WIDTH), F32)
    for br, parts in enumerate((o_c, o_s, o_w)):
        gx = gx_ref[br]
        gate = jnp.dot(ghi, gx, preferred_element_type=F32) + jnp.dot(glo, gx, preferred_element_type=F32)
        merged = jnp.concatenate([jnp.where(first, parts[0][g * Q_BLOCK:(g + 1) * Q_BLOCK],
                                            parts[1][g * Q_BLOCK:(g + 1) * Q_BLOCK])
                                  for g in range(NSA_GROUP)], axis=1)
        out = out + gate * merged
    o_ref[...] = out


def _attn_prompt(q, gt, kcmp, vcmp, ks, vs, kw, vw, ovt, oh, gx, n_batch, seq):
    nq = seq // Q_BLOCK
    n_cmp = kcmp.shape[1]
    tok = lambda b, i: (b * nq + i, 0)
    per_b = lambda b, i: (b, 0, 0)
    kv3 = lambda a: a.reshape(n_batch, seq, KV_W)
    return pl.pallas_call(
        functools.partial(_attn_prompt_kernel, seq=seq),
        grid=(n_batch, nq),
        in_specs=[pl.BlockSpec((Q_BLOCK, NSA_WIDTH), tok), pl.BlockSpec((Q_BLOCK, LANES), tok),
                  pl.BlockSpec((1, n_cmp, KV_W), per_b), pl.BlockSpec((1, n_cmp, KV_W), per_b),
                  pl.BlockSpec((1, seq, KV_W), per_b), pl.BlockSpec((1, seq, KV_W), per_b),
                  pl.BlockSpec((1, seq, KV_W), per_b), pl.BlockSpec((1, seq, KV_W), per_b),
                  _const_spec(ovt.shape), _const_spec(oh.shape), _const_spec(gx.shape)],
        out_specs=pl.BlockSpec((Q_BLOCK, NSA_WIDTH), tok),
        out_shape=jax.ShapeDtypeStruct((n_batch * seq, NSA_WIDTH), F32),
        scratch_shapes=[pltpu.VMEM((seq, 2 * KV_W), BF16), pltpu.VMEM((seq, KV_W), BF16),
                        pltpu.VMEM((seq, KV_W), BF16), pltpu.VMEM((seq, KV_W), BF16),
                        pltpu.VMEM((n_cmp, KV_W), BF16), pltpu.VMEM((n_cmp, KV_W), BF16)],
        compiler_params=pltpu.CompilerParams(dimension_semantics=("arbitrary", "arbitrary"),
                                             vmem_limit_bytes=VMEM_LIMIT),
        name="attn_prompt",
    )(q, gt, kcmp, vcmp, kv3(ks), kv3(vs), kv3(kw), kv3(vw), ovt, oh, gx)


SAMPLE_CHUNK_PAGES = 16


def _attn_sample_kernel(pt_ref, q_ref, gt_ref, kcmp_ref, vcmp_ref, ksn_ref, vsn_ref, kws_ref, vws_ref,
                        kwn_ref, vwn_ref, ov_ref, ex_ref, kpool_ref, vpool_ref,
                        o_ref, kbuf, vbuf, ksem, vsem, *, n_pages, past):
    b = pl.program_id(0)
    nb = pl.num_programs(0)
    n_chunks = n_pages // SAMPLE_CHUNK_PAGES
    chunk_blocks = SAMPLE_CHUNK_PAGES * PAGE_SIZE // SEL_BLOCK
    assert n_chunks % 2 == 0

    def copies(batch, chunk):
        slot = chunk % 2
        first = batch * n_pages + chunk * SAMPLE_CHUNK_PAGES
        return (_page_copies(pt_ref, kpool_ref, kbuf.at[slot], ksem.at[slot], first, SAMPLE_CHUNK_PAGES)
                + _page_copies(pt_ref, vpool_ref, vbuf.at[slot], vsem.at[slot], first, SAMPLE_CHUNK_PAGES))

    @pl.when(b == 0)
    def _():
        for c in copies(0, 0):
            c.start()

    ds = q_ref.shape[1]
    q = q_ref[0] * SCALE
    gt = gt_ref[0]
    per_head = NSA_GROUP * ds
    heads = range(NSA_KV_HEADS)
    hs = [slice(h * HEAD_DIM, (h + 1) * HEAD_DIM) for h in heads]
    qh = [jnp.concatenate([q[:, g * KV_W + h * HEAD_DIM:g * KV_W + (h + 1) * HEAD_DIM] for g in range(NSA_GROUP)],
                          axis=0).astype(BF16) for h in heads]
    rows = NSA_KV_HEADS * per_head
    trow = lax.broadcasted_iota(jnp.int32, (rows, 1), 0) & (ds - 1)
    qpos = past + trow

    def scores_t(kt):
        return jnp.concatenate([jnp.dot(qh[h], kt[h].astype(BF16), preferred_element_type=F32) for h in heads],
                               axis=0)

    def scores(k):
        return jnp.concatenate([_bdot_nt(qh[h], k[h]) for h in heads], axis=0)

    def weigh_t(vt):
        return lambda e: jnp.concatenate([_bdot_nt(e[h * per_head:(h + 1) * per_head], vt[h]) for h in heads],
                                         axis=0)

    def weigh(v):
        return lambda e: jnp.concatenate([_bdot(e[h * per_head:(h + 1) * per_head], v[h]) for h in heads], axis=0)

    n_cmp = kcmp_ref.shape[1] // NSA_KV_HEADS
    kc = kcmp_ref[0]
    vc = vcmp_ref[0]
    cmp_rows = [slice(h * n_cmp, (h + 1) * n_cmp) for h in heads]
    c = lax.broadcasted_iota(jnp.int32, (1, n_cmp), 1)
    p_c = _masked_softmax(scores([kc[r] for r in cmp_rows]), (c * CMP_STRIDE + (CMP_BLOCK - 1)) <= qpos)
    o_c = weigh([vc[r] for r in cmp_rows])(p_c)
    sums = []
    for h in heads:
        acc = p_c[h * per_head:h * per_head + ds]
        for g in range(1, NSA_GROUP):
            acc = acc + p_c[h * per_head + g * ds:h * per_head + (g + 1) * ds]
        sums.append(acc)
    imp = _split_dot(jnp.concatenate(sums, axis=0), ov_ref[...])
    qpos_sel = past + (lax.broadcasted_iota(jnp.int32, (NSA_KV_HEADS * ds, 1), 0) & (ds - 1))
    n_blocks = -(-(past + ds) // SEL_BLOCK)
    sel = _select_lanes(imp, qpos_sel // SEL_BLOCK, n_blocks)
    sel = jnp.concatenate([sel[h * ds:(h + 1) * ds] for h in heads for _ in range(NSA_GROUP)], axis=0)

    state = _softmax_start(rows, HEAD_DIM)
    for ch in range(n_chunks):
        if ch + 1 < n_chunks:
            for cp in copies(b, ch + 1):
                cp.start()
        else:
            @pl.when(b + 1 < nb)
            def _():
                for cp in copies(b + 1, 0):
                    cp.start()
        for cp in copies(b, ch):
            cp.wait()
        slot = ch % 2
        selm = jnp.dot(sel[:, ch * chunk_blocks:(ch + 1) * chunk_blocks], ex_ref[...], preferred_element_type=F32)
        state = _softmax_update(state, scores_t([kbuf[slot, h] for h in heads]), selm > 0.5,
                                weigh_t([vbuf[slot, h] for h in heads]))
    new_block = past // SEL_BLOCK
    tok = lax.broadcasted_iota(jnp.int32, (1, ds), 1)
    picked = sel[:, new_block:new_block + 1].astype(F32) > 0.5
    ksn = ksn_ref[0]
    vsn = vsn_ref[0]
    state = _softmax_update(state, scores([ksn[:, s] for s in hs]), picked & (tok <= trow),
                            weigh([vsn[:, s] for s in hs]))
    o_s = _softmax_finish(state)

    nwin = kws_ref.shape[2]
    state = _softmax_start(rows, HEAD_DIM)
    dp = qpos - (past - nwin + lax.broadcasted_iota(jnp.int32, (1, nwin), 1))
    state = _softmax_update(state, scores_t([kws_ref[h] for h in heads]), (dp >= 0) & (dp < WINDOW),
                            weigh_t([vws_ref[h] for h in heads]))
    dp = trow - tok
    kwn = kwn_ref[0]
    vwn = vwn_ref[0]
    state = _softmax_update(state, scores([kwn[:, s] for s in hs]), (dp >= 0) & (dp < WINDOW),
                            weigh([vwn[:, s] for s in hs]))
    o_w = _softmax_finish(state)

    pieces = []
    for g in range(NSA_GROUP):
        for h in heads:
            j = (h * NSA_GROUP + g) * N_BRANCH
            r = slice(h * per_head + g * ds, h * per_head + (g + 1) * ds)
            pieces.append(gt[:, j:j + 1] * o_c[r] + gt[:, j + 1:j + 2] * o_s[r] + gt[:, j + 2:j + 3] * o_w[r])
    o_ref[0] = jnp.concatenate(pieces, axis=1)


def _attn_sample(page_table, q, gt, kcmp, vcmp, ksn, vsn, kws, vws, win_first, kwn, vwn, ov, ex, kpool, vpool, past):
    n_batch, ds = q.shape[:2]
    n_pages = page_table.shape[0] // n_batch
    chunk_tokens = SAMPLE_CHUNK_PAGES * PAGE_SIZE
    per_b = lambda b, pt: (b, 0, 0)
    blk = lambda a: pl.BlockSpec((1,) + a.shape[1:], per_b)
    win = lambda a: pl.BlockSpec((NSA_KV_HEADS,) + a.shape[1:], lambda b, pt: (win_first + b, 0, 0))
    full = lambda a: pl.BlockSpec(a.shape, lambda b, pt: (0,) * a.ndim)
    slab_buf = pltpu.VMEM((2, NSA_KV_HEADS, HEAD_DIM, chunk_tokens), F32)
    grid_spec = pltpu.PrefetchScalarGridSpec(
        num_scalar_prefetch=1,
        grid=(n_batch,),
        in_specs=[blk(q), blk(gt), blk(kcmp), blk(vcmp), blk(ksn), blk(vsn), win(kws), win(vws), blk(kwn), blk(vwn),
                  full(ov), full(ex),
                  pl.BlockSpec(memory_space=pl.ANY), pl.BlockSpec(memory_space=pl.ANY)],
        out_specs=pl.BlockSpec((1, ds, NSA_WIDTH), per_b),
        scratch_shapes=[slab_buf, slab_buf, pltpu.SemaphoreType.DMA((2,)), pltpu.SemaphoreType.DMA((2,))],
    )
    return pl.pallas_call(
        functools.partial(_attn_sample_kernel, n_pages=n_pages, past=past),
        grid_spec=grid_spec,
        out_shape=jax.ShapeDtypeStruct((n_batch, ds, NSA_WIDTH), F32),
        compiler_params=pltpu.CompilerParams(dimension_semantics=("arbitrary",), vmem_limit_bytes=VMEM_LIMIT),
        name="attn_sample",
    )(page_table, q, gt, kcmp, vcmp, ksn, vsn, kws, vws, kwn, vwn, ov, ex, kpool, vpool)


CONV_ROWS = 64
FF_SPLIT = 2


def _conv_shifts(hpc, rs):
    off = hpc - (CONV_WIDTH - 1) * rs
    return sorted({(off + k * rs) % SUBLANES for k in range(CONV_WIDTH)} - {0})


def _merge_ffn_kernel(x_ref, o_ref, u_ref, cbuf_ref, fbuf_ref, cw_ref, cb_ref, lng_ref, lnb_ref, wout_ref,
                      fg_ref, wup_ref, fcw_ref, fcb_ref, wdn_ref,
                      y_ref, tail_ref, xx, xsh, cc, gg, *, rs):
    t = pl.program_id(1)
    tm = x_ref.shape[0]
    hpc = xx.shape[0] - tm
    hpf = gg.shape[0] - tm
    off_c = hpc - (CONV_WIDTH - 1) * rs
    off_f = hpf - (FFN_CONV_WIDTH - 1) * rs

    @pl.when(t == 0)
    def _():
        xx[pl.ds(0, hpc), :] = cbuf_ref[0]
        gg[pl.ds(0, hpf), :] = fbuf_ref[0]

    xx[pl.ds(hpc, tm), :] = u_ref[...]
    shifts = _conv_shifts(hpc, rs)
    span = hpc + tm - SUBLANES
    for idx, r in enumerate(shifts):
        xsh[idx, pl.ds(0, span), :] = xx[pl.ds(r, span), :]
    rc = min(CONV_ROWS, tm)
    for c0 in range(0, CONV_CH, LANES):
        for r0 in range(0, tm, rc):
            acc = jnp.zeros((rc, LANES), F32)
            for k in range(CONV_WIDTH):
                start = off_c + k * rs
                r = start % SUBLANES
                if r == 0:
                    tap = xx[pl.ds(start + r0, rc), c0:c0 + LANES]
                else:
                    tap = xsh[shifts.index(r), pl.ds(start - r + r0, rc), c0:c0 + LANES]
                acc = acc + cw_ref[k:k + 1, c0:c0 + LANES] * tap
            cc[pl.ds(r0, rc), c0:c0 + LANES] = acc
    y = cc[...] + cb_ref[...]
    mu = jnp.mean(y, -1, keepdims=True)
    var = jnp.mean(jnp.square(y - mu), -1, keepdims=True)
    y = (y - mu) * lax.rsqrt(var + EPS) * lng_ref[...] + lnb_ref[...]
    c = y * jax.nn.sigmoid(y)
    x1 = (x_ref[...] + _bdot(o_ref[...], wout_ref[0:NSA_WIDTH, :])
          + _bdot(c, wout_ref[NSA_WIDTH:NSA_WIDTH + CONV_CH, :]))
    h = (x1 * lax.rsqrt(jnp.mean(x1 * x1, -1, keepdims=True) + EPS) * fg_ref[...]).astype(BF16)
    fw = D_FF // FF_SPLIT
    f = jnp.zeros((tm, D_MODEL), F32)
    for s in range(FF_SPLIT):
        c0 = s * fw
        gg[pl.ds(hpf, tm), c0:c0 + fw] = jnp.dot(h, wup_ref[:, c0:c0 + fw], preferred_element_type=F32)
        up = jnp.dot(h, wup_ref[:, D_FF + c0:D_FF + c0 + fw], preferred_element_type=F32)
        gc = fcb_ref[:, c0:c0 + fw]
        for k in range(FFN_CONV_WIDTH):
            gc = gc + fcw_ref[k:k + 1, c0:c0 + fw] * gg[pl.ds(off_f + k * rs, tm), c0:c0 + fw]
        act = gc * jax.nn.sigmoid(gc) * up
        f = f + jnp.dot(act.astype(BF16), wdn_ref[c0:c0 + fw, :], preferred_element_type=F32)
    y_ref[...] = x1 + f
    tail_ref[0] = gg[pl.ds(tm, hpf), :]
    xx[pl.ds(0, hpc), :] = xx[pl.ds(tm, hpc), :]
    gg[pl.ds(0, hpf), :] = gg[pl.ds(tm, hpf), :]


def _merge_ffn(x, o, u, cbuf, fbuf, cw, cb, lng, lnb, wout, fg, wup, fcw, fcb, wdn, tm, rs):
    t = x.shape[0]
    n_batch = cbuf.shape[0]
    nt = t // n_batch // tm
    hpc = cbuf.shape[1]
    hpf = fbuf.shape[1]
    n_shift = len(_conv_shifts(hpc, rs))
    xsh_shape = (n_shift, hpc + tm - SUBLANES, CONV_CH) if n_shift else (1, SUBLANES, LANES)
    tok = lambda b, i: (b * nt + i, 0)
    per_b = lambda b, i: (b, 0, 0)
    return pl.pallas_call(
        functools.partial(_merge_ffn_kernel, rs=rs),
        grid=(n_batch, nt),
        in_specs=[pl.BlockSpec((tm, D_MODEL), tok), pl.BlockSpec((tm, NSA_WIDTH), tok),
                  pl.BlockSpec((tm, CONV_CH), tok),
                  pl.BlockSpec((1, hpc, CONV_CH), per_b), pl.BlockSpec((1, hpf, D_FF), per_b),
                  _const_spec(cw.shape), _const_spec(cb.shape), _const_spec(lng.shape), _const_spec(lnb.shape),
                  _const_spec(wout.shape), _const_spec(fg.shape), _const_spec(wup.shape),
                  _const_spec(fcw.shape), _const_spec(fcb.shape), _const_spec(wdn.shape)],
        out_specs=[pl.BlockSpec((tm, D_MODEL), tok), pl.BlockSpec((1, hpf, D_FF), per_b)],
        out_shape=[jax.ShapeDtypeStruct((t, D_MODEL), F32), jax.ShapeDtypeStruct((n_batch, hpf, D_FF), F32)],
        scratch_shapes=[pltpu.VMEM((hpc + tm, CONV_CH), F32), pltpu.VMEM(xsh_shape, F32),
                        pltpu.VMEM((tm, CONV_CH), F32), pltpu.VMEM((hpf + tm, D_FF), F32)],
        compiler_params=pltpu.CompilerParams(dimension_semantics=("arbitrary", "arbitrary"),
                                             vmem_limit_bytes=VMEM_LIMIT),
        name="merge_ffn",
    )(x, o, u, cbuf, fbuf, cw, cb, lng, lnb, wout, fg, wup, fcw, fcb, wdn)


def _rope_tables(pos):
    half = HEAD_DIM // 2
    inv = ROPE_THETA ** (-jnp.arange(half, dtype=F32) / half)
    ang = pos.astype(F32)[:, None] * inv[None, :]
    cos, sin = jnp.cos(ang), jnp.sin(ang)
    reps = KV_W // HEAD_DIM
    return (jnp.tile(jnp.concatenate([cos, cos], -1), (1, reps)),
            jnp.tile(jnp.concatenate([-sin, sin], -1), (1, reps)))


def _head_sum_matrix(width):
    i = jnp.arange(width) // HEAD_DIM
    return (i[:, None] == i[None, :]).astype(BF16)


def _overlap_matrix(n_cmp_rows, n_blocks, n_lanes):
    c0 = jnp.arange(n_cmp_rows) * CMP_STRIDE
    s0 = jnp.arange(n_lanes) * SEL_BLOCK
    ov = jnp.clip(jnp.minimum(c0[:, None] + CMP_BLOCK, s0[None, :] + SEL_BLOCK)
                  - jnp.maximum(c0[:, None], s0[None, :]), 0, CMP_BLOCK).astype(F32) / CMP_BLOCK
    ov = jnp.where(jnp.arange(n_lanes)[None, :] < n_blocks, ov, 0.0)
    return ov.astype(BF16)


def _expand_matrix(n_rows, n_blocks, n_keys):
    n = jnp.arange(n_rows)[:, None]
    t = jnp.arange(n_keys)[None, :] // SEL_BLOCK
    return ((n == t) & (n < n_blocks)).astype(BF16)


def _chunk_order_matrix():
    r = jnp.arange(PAGE_SIZE)
    src = (r % PAGE_CHUNKS) * CMP_STRIDE + r // PAGE_CHUNKS
    return (src[:, None] == jnp.arange(PAGE_SIZE)[None, :]).astype(BF16)


def _gate_expand_matrix():
    col = jnp.arange(NSA_WIDTH)
    head = ((col % KV_W) // HEAD_DIM) * NSA_GROUP + col // KV_W
    j = jnp.arange(LANES)[None, :, None]
    br = jnp.arange(N_BRANCH)[:, None, None]
    return (j == head[None, None, :] * N_BRANCH + br).astype(BF16)


def _pair_order(a, axis):
    shape = a.shape
    a = a.reshape(shape[:axis] + (NSA_KV_HEADS, NSA_GROUP, HEAD_DIM) + shape[axis + 1:])
    return jnp.swapaxes(a, axis, axis + 1).reshape(shape)


def _slabs(a):
    t = jnp.moveaxis(a, -3, -1)
    return t.reshape((-1,) + t.shape[-2:])


def _compress_rows_weights(w1, w2, pos):
    half = CMP_STRIDE * HEAD_DIM
    eye = jnp.eye(NSA_KV_HEADS, dtype=F32)

    def spread(wpart):
        w3 = wpart.reshape(CMP_STRIDE, HEAD_DIM, CMP_HIDDEN)
        return jnp.einsum("tdh,kj->tkdjh", w3, eye).reshape(CHUNK_W, NSA_KV_HEADS * CMP_HIDDEN)

    wab = jnp.concatenate([spread(w1[:half]), spread(w1[half:])], axis=1).astype(BF16)
    w2bd = jnp.einsum("hd,kj->khjd", w2, eye).reshape(NSA_KV_HEADS * CMP_HIDDEN, KV_W).astype(BF16)

    def pos_row(p):
        row = jnp.tile(p[:, None, :], (1, NSA_KV_HEADS, 1)).reshape(1, CHUNK_W)
        return jnp.concatenate([row, jnp.zeros((SUBLANES - 1, CHUNK_W), F32)], axis=0)

    posab = jnp.concatenate([pos_row(pos[:CMP_STRIDE]), pos_row(pos[CMP_STRIDE:])], axis=0)
    return wab, posab, w2bd


def _compress_pages_weights(w1, w2, pos):
    half = CMP_STRIDE * HEAD_DIM
    wt = jnp.concatenate([w1[:half].reshape(CMP_STRIDE, HEAD_DIM, CMP_HIDDEN),
                          w1[half:].reshape(CMP_STRIDE, HEAD_DIM, CMP_HIDDEN)], axis=2).astype(BF16)
    pos8 = jnp.concatenate([pos.reshape(1, CMP_BLOCK * HEAD_DIM),
                            jnp.zeros((SUBLANES - 1, CMP_BLOCK * HEAD_DIM), F32)], axis=0)
    return wt, w1.astype(BF16), pos8, w2.astype(BF16)


def _tile_lanes(v, reps):
    return jnp.tile(v, reps)[None, :]


PROMPT_TM = 512
FFN_TM = 256


def kernel(x_prompt, x_sample, cache_k_cmp, cache_v_cmp, cache_k_slc, cache_v_slc, state_k_win, state_v_win,
           state_conv, state_ffn_conv, page_table, attn_norm, w_in, q_norm, k_norm, kc_norm, cmp_pos,
           cmp_w1_k, cmp_w2_k, cmp_w1_v, cmp_w2_v, conv_w, conv_b, conv_ln_g, conv_ln_b, w_out, ffn_norm,
           w_up, ffn_conv_w, ffn_conv_b, w_down):
    B, S = x_prompt.shape[:2]
    DB, DS = x_sample.shape[:2]
    depth = w_in.shape[0]
    n_pool = cache_k_cmp.shape[1]
    n_pages = page_table.shape[1]
    past = n_pages * PAGE_SIZE
    win_buf = state_k_win.shape[2]
    p_win = min(WINDOW, S)
    TS = DB * DS

    cos_p, sin_p = _rope_tables(jnp.arange(S))
    cos_s, sin_s = _rope_tables(jnp.repeat(past + jnp.arange(DS), DB))
    bdq = _head_sum_matrix(NSA_WIDTH)
    bdk = _head_sum_matrix(KV_W)
    ncp = S // CMP_STRIDE
    ncs = past // CMP_STRIDE
    nb_p = S // SEL_BLOCK
    nb_s = -(-(past + DS) // SEL_BLOCK)
    nl_s = _round_up(nb_s, LANES)
    chunk_blocks = SAMPLE_CHUNK_PAGES * PAGE_SIZE // SEL_BLOCK
    ovt_p = _overlap_matrix(ncp, nb_p, LANES).T
    oh_p = _expand_matrix(LANES, nb_p, S).T
    gx = _gate_expand_matrix()
    ov_s = _overlap_matrix(ncs, nb_s, nl_s)
    ex_s = _expand_matrix(chunk_blocks, chunk_blocks, chunk_blocks * SEL_BLOCK)
    perm = _chunk_order_matrix()
    pt_flat = page_table.reshape(-1).astype(jnp.int32)

    o_g = NSA_WIDTH + 6 * KV_W
    w_cat = jnp.concatenate([_pair_order(w_in[:, :, :NSA_WIDTH], 2), w_in[:, :, NSA_WIDTH:o_g],
                             jnp.pad(w_in[:, :, o_g:o_g + GATE_COLS], ((0, 0), (0, 0), (0, LANES - GATE_COLS))),
                             w_in[:, :, o_g + GATE_COLS:]], axis=2).astype(BF16)
    w_out_b = jnp.concatenate([_pair_order(w_out[:, :NSA_WIDTH], 1), w_out[:, NSA_WIDTH:]], axis=1).astype(BF16)
    w_up_b = w_up.astype(BF16)
    w_down_b = w_down.astype(BF16)
    conv_w_p = jnp.pad(conv_w, ((0, 0), (0, _round_up(CONV_WIDTH, SUBLANES) - CONV_WIDTH), (0, 0)))
    fcw_p = jnp.pad(ffn_conv_w, ((0, 0), (0, SUBLANES - FFN_CONV_WIDTH), (0, 0)))

    kc_pool, vc_pool, ks_pool, vs_pool = (_slabs(a) for a in (cache_k_cmp, cache_v_cmp, cache_k_slc, cache_v_slc))
    kw_states, vw_states = _slabs(state_k_win), _slabs(state_v_win)

    hpc_p = _round_up(CONV_WIDTH - 1, SUBLANES)
    hpf_p = _round_up(FFN_CONV_WIDTH - 1, SUBLANES)
    conv0 = jnp.zeros((B, hpc_p, CONV_CH), F32)
    ffn0 = jnp.zeros((B, hpf_p, D_FF), F32)

    y_p = x_prompt.reshape(B * S, D_MODEL)
    y_s = x_sample.transpose(1, 0, 2).reshape(TS, D_MODEL)

    def to_bm(a):
        return a.reshape(DS, DB, a.shape[-1]).transpose(1, 0, 2)

    def heads(a, lead):
        return a.reshape(lead + (NSA_KV_HEADS, HEAD_DIM))

    outs = [[] for _ in range(16)]
    for l in range(depth):
        qg = _tile_lanes(q_norm[l], NSA_HEADS)
        kg = jnp.concatenate([_tile_lanes(k_norm[l, i], NSA_KV_HEADS) for i in range(N_BRANCH)], axis=1)
        kcg2 = _tile_lanes(kc_norm[l], NSA_KV_HEADS)
        kcg = kc_norm[l][None]
        rw_k = _compress_rows_weights(cmp_w1_k[l], cmp_w2_k[l], cmp_pos[l])
        rw_v = _compress_rows_weights(cmp_w1_v[l], cmp_w2_v[l], cmp_pos[l])
        pw_k = _compress_pages_weights(cmp_w1_k[l], cmp_w2_k[l], cmp_pos[l])
        pw_v = _compress_pages_weights(cmp_w1_v[l], cmp_w2_v[l], cmp_pos[l])
        merge_w = (conv_w_p[l], conv_b[l][None], conv_ln_g[l][None], conv_ln_b[l][None], w_out_b[l],
                   ffn_norm[l][None], w_up_b[l], fcw_p[l], ffn_conv_b[l][None], w_down_b[l])
        pt_l = pt_flat + l * n_pool

        q, kc, vc, ks, vs, kw, vw, gt, u = _inproj(y_p, attn_norm[l][None], w_cat[l], qg, kg, cos_p, sin_p, bdq,
                                                   PROMPT_TM)
        kcmp = _compress_rows(kc, *rw_k, kcg2, bdk, B, True)
        vcmp = _compress_rows(vc, *rw_v, kcg2, bdk, B, False)
        o = _attn_prompt(q, gt, kcmp, vcmp, ks, vs, kw, vw, ovt_p, oh_p, gx, B, S)
        y_p, tail = _merge_ffn(y_p, o, u, conv0, ffn0, *merge_w, FFN_TM, 1)
        for idx, a in enumerate((kc, vc, ks, vs)):
            outs[idx].append(heads(a, (B, S)))
        outs[4].append(heads(kw, (B, S))[:, -p_win:])
        outs[5].append(heads(vw, (B, S))[:, -p_win:])
        outs[6].append(u.reshape(B, S, CONV_CH)[:, -(CONV_WIDTH - 1):])
        outs[7].append(tail[:, -(FFN_CONV_WIDTH - 1):])

        q, kc, vc, ks, vs, kw, vw, gt, u = _inproj(y_s, attn_norm[l][None], w_cat[l], qg, kg, cos_s, sin_s, bdq, TS)
        kcmp = _compress_pages(pt_l, kc_pool, perm, *pw_k, kcg, DB, True)
        vcmp = _compress_pages(pt_l, vc_pool, perm, *pw_v, kcg, DB, False)
        kc, vc, ks, vs, kw, vw, u_bm = (to_bm(a) for a in (kc, vc, ks, vs, kw, vw, u))
        o = _attn_sample(pt_l, to_bm(q), to_bm(gt), kcmp, vcmp, ks, vs, kw_states, vw_states, l * DB, kw, vw,
                         ov_s, ex_s, ks_pool, vs_pool, past)
        o_tm = o.transpose(1, 0, 2).reshape(TS, NSA_WIDTH)
        cbuf = state_conv[l].transpose(1, 0, 2).reshape(1, (CONV_WIDTH - 1) * DB, CONV_CH)
        fbuf = state_ffn_conv[l].transpose(1, 0, 2).reshape(1, (FFN_CONV_WIDTH - 1) * DB, D_FF)
        y_s, tail = _merge_ffn(y_s, o_tm, u, cbuf, fbuf, *merge_w, TS, DB)
        for idx, a in enumerate((kc, vc, ks, vs)):
            outs[8 + idx].append(heads(a, (DB, DS)))
        outs[12].append(jnp.concatenate([state_k_win[l], heads(kw, (DB, DS))], 1)[:, -win_buf:])
        outs[13].append(jnp.concatenate([state_v_win[l], heads(vw, (DB, DS))], 1)[:, -win_buf:])
        outs[14].append(jnp.concatenate([state_conv[l], u_bm], 1)[:, -(CONV_WIDTH - 1):])
        outs[15].append(tail.reshape(FFN_CONV_WIDTH - 1, DB, D_FF).transpose(1, 0, 2))

    y_prompt = y_p.reshape(B, S, D_MODEL)
    y_sample = y_s.reshape(DS, DB, D_MODEL).transpose(1, 0, 2)
    return (y_prompt, y_sample) + tuple(jnp.stack(o) for o in outs)
```

```python
import functools

import jax
import jax.numpy as jnp
from jax import lax
from jax.experimental import pallas as pl
from jax.experimental.pallas import tpu as pltpu

F32 = jnp.float32
BF16 = jnp.bfloat16

D_MODEL = 1024
HEAD_DIM = 64
NSA_HEADS = 8
NSA_KV_HEADS = 2
NSA_GROUP = NSA_HEADS // NSA_KV_HEADS
NSA_WIDTH = NSA_HEADS * HEAD_DIM
KV_W = NSA_KV_HEADS * HEAD_DIM
CONV_CH = D_MODEL - NSA_WIDTH
N_BRANCH = 3
CMP_BLOCK = 32
CMP_STRIDE = 16
CMP_HIDDEN = 128
SEL_BLOCK = 64
N_SELECT = 16
WINDOW = 512
CONV_WIDTH = 31
D_FF = 2816
FFN_CONV_WIDTH = 3
ROPE_THETA = 10000.0
EPS = 1e-6
Q_BLOCK = 128
GATE_COLS = NSA_HEADS * N_BRANCH
PAGE_SIZE = 128
SCALE = HEAD_DIM ** -0.5
NEG = -1e30

LANES = 128
SUBLANES = 8
VMEM_LIMIT = 52 * 1024 * 1024
CHUNK_W = CMP_STRIDE * KV_W
PAGE_CHUNKS = PAGE_SIZE // CMP_STRIDE
Z_COLS = NSA_WIDTH + 6 * KV_W + LANES + 2 * CONV_CH
NT_DIMS = (((1,), (1,)), ((), ()))

assert NSA_KV_HEADS == 2 and KV_W == LANES and PAGE_SIZE == LANES and CMP_BLOCK == 2 * CMP_STRIDE


def _round_up(x, m):
    return (x + m - 1) // m * m


def _bdot(a, b):
    return jnp.dot(a.astype(BF16), b.astype(BF16), preferred_element_type=F32)


def _bdot_nt(a, b):
    return lax.dot_general(a.astype(BF16), b.astype(BF16), NT_DIMS, preferred_element_type=F32)


def _split(a):
    hi = a.astype(BF16)
    return hi, (a - hi.astype(F32)).astype(BF16)


def _split_dot(a, b_bf16):
    hi, lo = _split(a)
    return (jnp.dot(hi, b_bf16, preferred_element_type=F32)
            + jnp.dot(lo, b_bf16, preferred_element_type=F32))


def _head_rms(x, bd, gain):
    ms = _split_dot(x * x, bd) * (1.0 / HEAD_DIM)
    return x * lax.rsqrt(ms + EPS) * gain


def _rope(y, cos, sin_signed):
    w = y.shape[-1]
    half = HEAD_DIM // 2
    fwd = pltpu.roll(y, w - half, 1)
    bwd = pltpu.roll(y, half, 1)
    lane = lax.broadcasted_iota(jnp.int32, y.shape, 1)
    swapped = jnp.where((lane & (HEAD_DIM - 1)) < half, fwd, bwd)
    return y * cos + swapped * sin_signed


def _masked_softmax(s, mask):
    s = jnp.where(mask, s, NEG)
    m = jnp.max(s, -1, keepdims=True)
    e = jnp.where(mask, jnp.exp(s - m), 0.0)
    return e / jnp.maximum(jnp.sum(e, -1, keepdims=True), 1e-30)


def _softmax_start(rows, width):
    return jnp.full((rows, 1), NEG, F32), jnp.zeros((rows, 1), F32), jnp.zeros((rows, width), F32)


def _softmax_update(state, s, mask, weigh):
    m_i, l_i, acc = state
    s = jnp.where(mask, s, NEG)
    m_new = jnp.maximum(m_i, jnp.max(s, -1, keepdims=True))
    alpha = jnp.exp(m_i - m_new)
    e = jnp.where(mask, jnp.exp(s - m_new), 0.0)
    return m_new, alpha * l_i + jnp.sum(e, -1, keepdims=True), alpha * acc + weigh(e)


def _softmax_finish(state):
    _, l_i, acc = state
    return acc / jnp.maximum(l_i, 1e-30)


def _const_spec(shape):
    nd = len(shape)
    return pl.BlockSpec(shape, lambda *_: (0,) * nd, pipeline_mode=pl.Buffered(1))


def _inproj_kernel(x_ref, g_ref, w_ref, qg_ref, kg_ref, cos_ref, sin_ref, bd_ref,
                   q_ref, kc_ref, vc_ref, ks_ref, vs_ref, kw_ref, vw_ref, gt_ref, u_ref):
    x = x_ref[...]
    h = x * lax.rsqrt(jnp.mean(x * x, -1, keepdims=True) + EPS) * g_ref[...]
    z = jnp.dot(h.astype(BF16), w_ref[...], preferred_element_type=F32)
    cos = cos_ref[...]
    sin = sin_ref[...]
    bdq = bd_ref[...]
    bdk = bdq[:KV_W, :KV_W]
    nrep = NSA_WIDTH // KV_W
    q = _head_rms(z[:, :NSA_WIDTH], bdq, qg_ref[...])
    q_ref[...] = _rope(q, jnp.concatenate([cos] * nrep, axis=1), jnp.concatenate([sin] * nrep, axis=1))
    kg = kg_ref[...]
    o = NSA_WIDTH
    kc_ref[...] = _rope(_head_rms(z[:, o:o + KV_W], bdk, kg[:, 0:KV_W]), cos, sin)
    vc_ref[...] = z[:, o + KV_W:o + 2 * KV_W]
    ks_ref[...] = _rope(_head_rms(z[:, o + 2 * KV_W:o + 3 * KV_W], bdk, kg[:, KV_W:2 * KV_W]), cos, sin)
    vs_ref[...] = z[:, o + 3 * KV_W:o + 4 * KV_W]
    kw_ref[...] = _rope(_head_rms(z[:, o + 4 * KV_W:o + 5 * KV_W], bdk, kg[:, 2 * KV_W:3 * KV_W]), cos, sin)
    vw_ref[...] = z[:, o + 5 * KV_W:o + 6 * KV_W]
    o += 6 * KV_W
    gt_ref[...] = jax.nn.sigmoid(z[:, o:o + LANES])
    o += LANES
    u_ref[...] = z[:, o:o + CONV_CH] * jax.nn.sigmoid(z[:, o + CONV_CH:o + 2 * CONV_CH])


def _inproj(x, g, w, qg, kg, cos, sin, bd, tm):
    t = x.shape[0]
    nt = t // tm
    npos = cos.shape[0] // tm
    row = lambda i: (i, 0)
    pos = lambda i: (i % npos, 0)
    widths = [NSA_WIDTH] + [KV_W] * 6 + [LANES, CONV_CH]
    return pl.pallas_call(
        _inproj_kernel,
        grid=(nt,),
        in_specs=[pl.BlockSpec((tm, D_MODEL), row), _const_spec((1, D_MODEL)), _const_spec((D_MODEL, Z_COLS)),
                  _const_spec((1, NSA_WIDTH)), _const_spec((1, 3 * KV_W)),
                  pl.BlockSpec((tm, KV_W), pos), pl.BlockSpec((tm, KV_W), pos),
                  _const_spec((NSA_WIDTH, NSA_WIDTH))],
        out_specs=[pl.BlockSpec((tm, wd), row) for wd in widths],
        out_shape=[jax.ShapeDtypeStruct((t, wd), F32) for wd in widths],
        compiler_params=pltpu.CompilerParams(dimension_semantics=("arbitrary",), vmem_limit_bytes=VMEM_LIMIT),
        name="inproj",
    )(x, g, w, qg, kg, cos, sin, bd)


def _compress_rows_kernel(x_ref, wab_ref, pos_ref, w2_ref, g_ref, bd_ref, out_ref, *, norm):
    x = x_ref[0]
    nch = x.shape[0]
    hid2 = NSA_KV_HEADS * CMP_HIDDEN
    wab = wab_ref[...]
    z = jnp.dot(x.astype(BF16), wab, preferred_element_type=F32)
    pos = pos_ref[...].astype(BF16)
    bias = (jnp.dot(pos[0:SUBLANES], wab[:, :hid2], preferred_element_type=F32)
            + jnp.dot(pos[SUBLANES:], wab[:, hid2:], preferred_element_type=F32))[0:1]
    nxt = pltpu.roll(z[:, hid2:], nch - 1, 0)
    hid = jax.nn.gelu(z[:, :hid2] + nxt + bias)
    y = jnp.dot(hid.astype(BF16), w2_ref[...], preferred_element_type=F32)
    if norm:
        y = _head_rms(y, bd_ref[...], g_ref[...])
    rowi = lax.broadcasted_iota(jnp.int32, y.shape, 0)
    out_ref[0] = jnp.where(rowi < nch - 1, y, 0.0)


def _compress_rows(rows, wab, pos, w2, g, bd, n_batch, norm):
    x = rows.reshape(n_batch, -1, CHUNK_W)
    nch = x.shape[1]
    return pl.pallas_call(
        functools.partial(_compress_rows_kernel, norm=norm),
        grid=(n_batch,),
        in_specs=[pl.BlockSpec((1, nch, CHUNK_W), lambda b: (b, 0, 0)), _const_spec(wab.shape),
                  _const_spec(pos.shape), _const_spec(w2.shape), _const_spec(g.shape), _const_spec(bd.shape)],
        out_specs=pl.BlockSpec((1, nch, KV_W), lambda b: (b, 0, 0)),
        out_shape=jax.ShapeDtypeStruct((n_batch, nch, KV_W), F32),
        compiler_params=pltpu.CompilerParams(dimension_semantics=("arbitrary",), vmem_limit_bytes=VMEM_LIMIT),
        name="compress_rows",
    )(x, wab, pos, w2, g, bd)


def _page_copies(pt_ref, pool_ref, dst, sem, first, n_pages):
    copies = []
    for p in range(n_pages):
        page = pt_ref[first + p]
        for h in range(NSA_KV_HEADS):
            copies.append(pltpu.make_async_copy(pool_ref.at[page * NSA_KV_HEADS + h],
                                                dst.at[h, :, pl.ds(p * PAGE_SIZE, PAGE_SIZE)], sem))
    return copies


REGROUP_UNROLL = 8


def _compress_pages_kernel(pt_ref, pool_ref, perm_ref, wt_ref, w1_ref, pos_ref, w2_ref, g_ref, out_ref,
                           buf, xg, sem, *, n_pages, norm):
    b = pl.program_id(0)
    nb = pl.num_programs(0)

    def copies(batch, slot):
        return _page_copies(pt_ref, pool_ref, buf.at[slot], sem.at[slot], batch * n_pages, n_pages)

    @pl.when(b == 0)
    def _():
        for c in copies(0, 0):
            c.start()

    @pl.when(b + 1 < nb)
    def _():
        for c in copies(b + 1, (b + 1) % 2):
            c.start()

    slot = b % 2
    for c in copies(b, slot):
        c.wait()

    perm = perm_ref[...]
    zero = jnp.zeros((HEAD_DIM, PAGE_SIZE), BF16)

    def regroup(i, carry):
        for k in range(REGROUP_UNROLL):
            p = i * REGROUP_UNROLL + k
            t0 = pl.multiple_of(p * PAGE_SIZE, PAGE_SIZE)
            for h in range(NSA_KV_HEADS):
                slab = buf[slot, h, :, pl.ds(t0, PAGE_SIZE)].astype(BF16)
                twice = jnp.concatenate([jnp.concatenate([slab, zero], axis=1),
                                         jnp.concatenate([zero, slab], axis=1)], axis=0)
                xg[h, p] = lax.dot_general(perm, twice, NT_DIMS, preferred_element_type=F32)
        return carry

    lax.fori_loop(0, n_pages // REGROUP_UNROLL, regroup, 0)

    nch = n_pages * PAGE_CHUNKS
    acc = jnp.zeros((NSA_KV_HEADS * nch, 2 * CMP_HIDDEN), F32)
    for u in range(CMP_STRIDE // 2):
        x = jnp.concatenate([xg[h, :, pl.ds(u * PAGE_CHUNKS, PAGE_CHUNKS), :].reshape(nch, 2 * HEAD_DIM)
                             for h in range(NSA_KV_HEADS)], axis=0)
        acc = acc + jnp.dot(x.astype(BF16), wt_ref[u], preferred_element_type=F32)
    bias = jnp.dot(pos_ref[...].astype(BF16), w1_ref[...], preferred_element_type=F32)[0:1]
    nxt = pltpu.roll(acc[:, CMP_HIDDEN:], NSA_KV_HEADS * nch - 1, 0)
    hid = jax.nn.gelu(acc[:, :CMP_HIDDEN] + nxt + bias)
    y = jnp.dot(hid.astype(BF16), w2_ref[...], preferred_element_type=F32)
    if norm:
        y = y * lax.rsqrt(jnp.mean(y * y, -1, keepdims=True) + EPS) * g_ref[...]
    rowi = lax.broadcasted_iota(jnp.int32, y.shape, 0)
    last = (rowi == nch - 1) | (rowi == NSA_KV_HEADS * nch - 1)
    out_ref[0] = jnp.where(last, 0.0, y)


def _compress_pages(page_table, pool, perm, wt, w1, pos, w2, g, n_batch, norm):
    n_pages = page_table.shape[0] // n_batch
    nch = n_pages * PAGE_CHUNKS
    full = lambda a: pl.BlockSpec(a.shape, lambda b, pt: (0,) * a.ndim)
    grid_spec = pltpu.PrefetchScalarGridSpec(
        num_scalar_prefetch=1,
        grid=(n_batch,),
        in_specs=[pl.BlockSpec(memory_space=pl.ANY), full(perm), full(wt), full(w1), full(pos), full(w2), full(g)],
        out_specs=pl.BlockSpec((1, NSA_KV_HEADS * nch, HEAD_DIM), lambda b, pt: (b, 0, 0)),
        scratch_shapes=[pltpu.VMEM((2, NSA_KV_HEADS, HEAD_DIM, n_pages * PAGE_SIZE), F32),
                        pltpu.VMEM((NSA_KV_HEADS, n_pages, PAGE_SIZE // 2, 2 * HEAD_DIM), F32),
                        pltpu.SemaphoreType.DMA((2,))],
    )
    return pl.pallas_call(
        functools.partial(_compress_pages_kernel, n_pages=n_pages, norm=norm),
        grid_spec=grid_spec,
        out_shape=jax.ShapeDtypeStruct((n_batch, NSA_KV_HEADS * nch, HEAD_DIM), F32),
        compiler_params=pltpu.CompilerParams(dimension_semantics=("arbitrary",), vmem_limit_bytes=VMEM_LIMIT),
        name="compress_pages",
    )(page_table, pool, perm, wt, w1, pos, w2, g)


def _select_lanes(imp, cur, n_blocks):
    n = lax.broadcasted_iota(jnp.int32, (1, imp.shape[1]), 1)
    forced = (n == 0) | (n == cur) | (n == cur - 1)
    v = jnp.where(forced, jnp.inf, jnp.where(n <= cur, imp, -jnp.inf))
    rank = jnp.zeros(imp.shape, jnp.int32)
    for m in range(n_blocks):
        vm = v[:, m:m + 1]
        before = (n > m).astype(jnp.int32)
        rank = rank + jnp.where(vm > v, 1, jnp.where(vm == v, before, 0))
    return jnp.where(rank < min(N_SELECT, n_blocks), 1.0, 0.0).astype(BF16)


def _select_rows(imp_t, cur, n_blocks):
    n = lax.broadcasted_iota(jnp.int32, imp_t.shape, 0)
    forced = (n == 0) | (n == cur) | (n == cur - 1)
    v = jnp.where(forced, jnp.inf, jnp.where(n <= cur, imp_t, -jnp.inf))
    rank = jnp.zeros(imp_t.shape, jnp.int32)
    for m in range(n_blocks):
        vm = v[m:m + 1, :]
        rank = rank + jnp.where(vm > v, 1, jnp.where(vm == v, (n > m).astype(jnp.int32), 0))
    return jnp.where(rank < min(N_SELECT, n_blocks), 1.0, 0.0)


KEY_BLOCK = 512


def _attn_prompt_kernel(q_ref, gt_ref, kcmp_ref, vcmp_ref, ks_ref, vs_ref, kw_ref, vw_ref, ovt_ref, oh_ref, gx_ref,
                        o_ref, ksa, vsb, kwb, vwb, kcb, vcb, *, seq):
    i = pl.program_id(1)

    @pl.when(i == 0)
    def _():
        ksa[:, 0:KV_W] = ks_ref[0].astype(BF16)
        ksa[:, KV_W:2 * KV_W] = oh_ref[...]
        vsb[...] = vs_ref[0].astype(BF16)
        kwb[...] = kw_ref[0].astype(BF16)
        vwb[...] = vw_ref[0].astype(BF16)
        kcb[...] = kcmp_ref[0].astype(BF16)
        vcb[...] = vcmp_ref[0].astype(BF16)

    qs = i * Q_BLOCK
    q = q_ref[...] * SCALE
    rows = NSA_GROUP * Q_BLOCK
    lane = lax.broadcasted_iota(jnp.int32, (1, KV_W), 1)
    qpos4 = qs + (lax.broadcasted_iota(jnp.int32, (rows, 1), 0) & (Q_BLOCK - 1))
    qpos_l = qs + lax.broadcasted_iota(jnp.int32, (1, Q_BLOCK), 1)
    n_cmp = kcb.shape[0]
    n_blocks = seq // SEL_BLOCK
    win_keys = WINDOW + Q_BLOCK
    n_kb = (qs + Q_BLOCK + KEY_BLOCK - 1) // KEY_BLOCK
    w0 = pl.multiple_of(jnp.maximum(qs - WINDOW, 0), Q_BLOCK)
    o_c, o_w, q4a = [], [], []
    for h in range(NSA_KV_HEADS):
        in_head = (lane >= h * HEAD_DIM) & (lane < (h + 1) * HEAD_DIM)
        q4 = jnp.concatenate([jnp.where(in_head, q[:, g * KV_W:(g + 1) * KV_W], 0.0) for g in range(NSA_GROUP)],
                             axis=0).astype(BF16)
        s_c = lax.dot_general(q4, kcb[...], NT_DIMS, preferred_element_type=F32)
        c = lax.broadcasted_iota(jnp.int32, (1, n_cmp), 1)
        p_c = _masked_softmax(s_c, (c * CMP_STRIDE + (CMP_BLOCK - 1)) <= qpos4)
        o_c.append(jnp.dot(p_c.astype(BF16), vcb[...], preferred_element_type=F32))
        p_sum = p_c[0:Q_BLOCK]
        for g in range(1, NSA_GROUP):
            p_sum = p_sum + p_c[g * Q_BLOCK:(g + 1) * Q_BLOCK]
        hi, lo = _split(p_sum)
        ovt = ovt_ref[...]
        imp_t = (lax.dot_general(ovt, hi, NT_DIMS, preferred_element_type=F32)
                 + lax.dot_general(ovt, lo, NT_DIMS, preferred_element_type=F32))[0:n_blocks]
        sel_t = _select_rows(imp_t, qpos_l // SEL_BLOCK, n_blocks)
        bias_t = jnp.concatenate([jnp.where(sel_t > 0.5, 0.0, NEG),
                                  jnp.zeros((KV_W - n_blocks, Q_BLOCK), F32)], axis=0)
        bias = bias_t.T.astype(BF16)
        q4a.append(jnp.concatenate([q4, jnp.concatenate([bias] * NSA_GROUP, axis=0)], axis=1))
        s_w = lax.dot_general(q4, kwb[pl.ds(w0, win_keys), :], NT_DIMS, preferred_element_type=F32)
        dp = qpos4 - (w0 + lax.broadcasted_iota(jnp.int32, (1, win_keys), 1))
        s_w = jnp.where((dp >= 0) & (dp < WINDOW), s_w, NEG)
        e_w = jnp.exp(s_w - jnp.max(s_w, -1, keepdims=True))
        acc_w = jnp.dot(e_w.astype(BF16), vwb[pl.ds(w0, win_keys), :], preferred_element_type=F32)
        o_w.append(acc_w * (1.0 / jnp.sum(e_w, -1, keepdims=True)))

    def sel_block(j, carry, causal):
        k0 = pl.multiple_of(j * KEY_BLOCK, KEY_BLOCK)
        kb = ksa[pl.ds(k0, KEY_BLOCK), :]
        vb = vsb[pl.ds(k0, KEY_BLOCK), :]
        new = []
        for h in range(NSA_KV_HEADS):
            m_i, l_i, acc = carry[h]
            s = lax.dot_general(q4a[h], kb, NT_DIMS, preferred_element_type=F32)
            if causal:
                tok = k0 + lax.broadcasted_iota(jnp.int32, (1, KEY_BLOCK), 1)
                s = jnp.where(tok <= qpos4, s, NEG)
            m_new = jnp.maximum(m_i, jnp.max(s, -1, keepdims=True))
            alpha = jnp.exp(m_i - m_new)
            e = jnp.exp(s - m_new)
            l_new = alpha * l_i + jnp.sum(e, -1, keepdims=True)
            new.append((m_new, l_new, alpha * acc + jnp.dot(e.astype(BF16), vb, preferred_element_type=F32)))
        return tuple(new)

    carry = tuple(_softmax_start(rows, KV_W) for _ in range(NSA_KV_HEADS))
    carry = lax.fori_loop(0, n_kb - 1, functools.partial(sel_block, causal=False), carry)
    carry = sel_block(n_kb - 1, carry, True)
    o_s = [acc_s * (1.0 / l_s) for _, l_s, acc_s in carry]

    first = lane < HEAD_DIM
    ghi, glo = _split(gt_ref[...])
    out = jnp.zeros((Q_BLOCK, NSA_WIDTH), F32)
    for br, parts in enumerate((o_c, o_s, o_w)):
        gx = gx_ref[br]
        gate = jnp.dot(ghi, gx, preferred_element_type=F32) + jnp.dot(glo, gx, preferred_element_type=F32)
        merged = jnp.concatenate([jnp.where(first, parts[0][g * Q_BLOCK:(g + 1) * Q_BLOCK],
                                            parts[1][g * Q_BLOCK:(g + 1) * Q_BLOCK])
                                  for g in range(NSA_GROUP)], axis=1)
        out = out + gate * merged
    o_ref[...] = out


def _attn_prompt(q, gt, kcmp, vcmp, ks, vs, kw, vw, ovt, oh, gx, n_batch, seq):
    nq = seq // Q_BLOCK
    n_cmp = kcmp.shape[1]
    tok = lambda b, i: (b * nq + i, 0)
    per_b = lambda b, i: (b, 0, 0)
    kv3 = lambda a: a.reshape(n_batch, seq, KV_W)
    return pl.pallas_call(
        functools.partial(_attn_prompt_kernel, seq=seq),
        grid=(n_batch, nq),
        in_specs=[pl.BlockSpec((Q_BLOCK, NSA_WIDTH), tok), pl.BlockSpec((Q_BLOCK, LANES), tok),
                  pl.BlockSpec((1, n_cmp, KV_W), per_b), pl.BlockSpec((1, n_cmp, KV_W), per_b),
                  pl.BlockSpec((1, seq, KV_W), per_b), pl.BlockSpec((1, seq, KV_W), per_b),
                  pl.BlockSpec((1, seq, KV_W), per_b), pl.BlockSpec((1, seq, KV_W), per_b),
                  _const_spec(ovt.shape), _const_spec(oh.shape), _const_spec(gx.shape)],
        out_specs=pl.BlockSpec((Q_BLOCK, NSA_WIDTH), tok),
        out_shape=jax.ShapeDtypeStruct((n_batch * seq, NSA_WIDTH), F32),
        scratch_shapes=[pltpu.VMEM((seq, 2 * KV_W), BF16), pltpu.VMEM((seq, KV_W), BF16),
                        pltpu.VMEM((seq, KV_W), BF16), pltpu.VMEM((seq, KV_W), BF16),
                        pltpu.VMEM((n_cmp, KV_W), BF16), pltpu.VMEM((n_cmp, KV_W), BF16)],
        compiler_params=pltpu.CompilerParams(dimension_semantics=("arbitrary", "arbitrary"),
                                             vmem_limit_bytes=VMEM_LIMIT),
        name="attn_prompt",
    )(q, gt, kcmp, vcmp, kv3(ks), kv3(vs), kv3(kw), kv3(vw), ovt, oh, gx)


SAMPLE_CHUNK_PAGES = 16


def _attn_sample_kernel(pt_ref, q_ref, gt_ref, kcmp_ref, vcmp_ref, ksn_ref, vsn_ref, kws_ref, vws_ref,
                        kwn_ref, vwn_ref, ov_ref, ex_ref, kpool_ref, vpool_ref,
                        o_ref, kbuf, vbuf, ksem, vsem, *, n_pages, past):
    b = pl.program_id(0)
    nb = pl.num_programs(0)
    n_chunks = n_pages // SAMPLE_CHUNK_PAGES
    chunk_blocks = SAMPLE_CHUNK_PAGES * PAGE_SIZE // SEL_BLOCK
    assert n_chunks % 2 == 0

    def copies(batch, chunk):
        slot = chunk % 2
        first = batch * n_pages + chunk * SAMPLE_CHUNK_PAGES
        return (_page_copies(pt_ref, kpool_ref, kbuf.at[slot], ksem.at[slot], first, SAMPLE_CHUNK_PAGES)
                + _page_copies(pt_ref, vpool_ref, vbuf.at[slot], vsem.at[slot], first, SAMPLE_CHUNK_PAGES))

    @pl.when(b == 0)
    def _():
        for c in copies(0, 0):
            c.start()

    ds = q_ref.shape[1]
    q = q_ref[0] * SCALE
    gt = gt_ref[0]
    per_head = NSA_GROUP * ds
    heads = range(NSA_KV_HEADS)
    hs = [slice(h * HEAD_DIM, (h + 1) * HEAD_DIM) for h in heads]
    qh = [jnp.concatenate([q[:, g * KV_W + h * HEAD_DIM:g * KV_W + (h + 1) * HEAD_DIM] for g in range(NSA_GROUP)],
                          axis=0).astype(BF16) for h in heads]
    rows = NSA_KV_HEADS * per_head
    trow = lax.broadcasted_iota(jnp.int32, (rows, 1), 0) & (ds - 1)
    qpos = past + trow

    def scores_t(kt):
        return jnp.concatenate([jnp.dot(qh[h], kt[h].astype(BF16), preferred_element_type=F32) for h in heads],
                               axis=0)

    def scores(k):
        return jnp.concatenate([_bdot_nt(qh[h], k[h]) for h in heads], axis=0)

    def weigh_t(vt):
        return lambda e: jnp.concatenate([_bdot_nt(e[h * per_head:(h + 1) * per_head], vt[h]) for h in heads],
                                         axis=0)

    def weigh(v):
        return lambda e: jnp.concatenate([_bdot(e[h * per_head:(h + 1) * per_head], v[h]) for h in heads], axis=0)

    n_cmp = kcmp_ref.shape[1] // NSA_KV_HEADS
    kc = kcmp_ref[0]
    vc = vcmp_ref[0]
    cmp_rows = [slice(h * n_cmp, (h + 1) * n_cmp) for h in heads]
    c = lax.broadcasted_iota(jnp.int32, (1, n_cmp), 1)
    p_c = _masked_softmax(scores([kc[r] for r in cmp_rows]), (c * CMP_STRIDE + (CMP_BLOCK - 1)) <= qpos)
    o_c = weigh([vc[r] for r in cmp_rows])(p_c)
    sums = []
    for h in heads:
        acc = p_c[h * per_head:h * per_head + ds]
        for g in range(1, NSA_GROUP):
            acc = acc + p_c[h * per_head + g * ds:h * per_head + (g + 1) * ds]
        sums.append(acc)
    imp = _split_dot(jnp.concatenate(sums, axis=0), ov_ref[...])
    qpos_sel = past + (lax.broadcasted_iota(jnp.int32, (NSA_KV_HEADS * ds, 1), 0) & (ds - 1))
    n_blocks = -(-(past + ds) // SEL_BLOCK)
    sel = _select_lanes(imp, qpos_sel // SEL_BLOCK, n_blocks)
    sel = jnp.concatenate([sel[h * ds:(h + 1) * ds] for h in heads for _ in range(NSA_GROUP)], axis=0)

    state = _softmax_start(rows, HEAD_DIM)
    for ch in range(n_chunks):
        if ch + 1 < n_chunks:
            for cp in copies(b, ch + 1):
                cp.start()
        else:
            @pl.when(b + 1 < nb)
            def _():
                for cp in copies(b + 1, 0):
                    cp.start()
        for cp in copies(b, ch):
            cp.wait()
        slot = ch % 2
        selm = jnp.dot(sel[:, ch * chunk_blocks:(ch + 1) * chunk_blocks], ex_ref[...], preferred_element_type=F32)
        state = _softmax_update(state, scores_t([kbuf[slot, h] for h in heads]), selm > 0.5,
                                weigh_t([vbuf[slot, h] for h in heads]))
    new_block = past // SEL_BLOCK
    tok = lax.broadcasted_iota(jnp.int32, (1, ds), 1)
    picked = sel[:, new_block:new_block + 1].astype(F32) > 0.5
    ksn = ksn_ref[0]
    vsn = vsn_ref[0]
    state = _softmax_update(state, scores([ksn[:, s] for s in hs]), picked & (tok <= trow),
                            weigh([vsn[:, s] for s in hs]))
    o_s = _softmax_finish(state)

    nwin = kws_ref.shape[2]
    state = _softmax_start(rows, HEAD_DIM)
    dp = qpos - (past - nwin + lax.broadcasted_iota(jnp.int32, (1, nwin), 1))
    state = _softmax_update(state, scores_t([kws_ref[h] for h in heads]), (dp >= 0) & (dp < WINDOW),
                            weigh_t([vws_ref[h] for h in heads]))
    dp = trow - tok
    kwn = kwn_ref[0]
    vwn = vwn_ref[0]
    state = _softmax_update(state, scores([kwn[:, s] for s in hs]), (dp >= 0) & (dp < WINDOW),
                            weigh([vwn[:, s] for s in hs]))
    o_w = _softmax_finish(state)

    pieces = []
    for g in range(NSA_GROUP):
        for h in heads:
            j = (h * NSA_GROUP + g) * N_BRANCH
            r = slice(h * per_head + g * ds, h * per_head + (g + 1) * ds)
            pieces.append(gt[:, j:j + 1] * o_c[r] + gt[:, j + 1:j + 2] * o_s[r] + gt[:, j + 2:j + 3] * o_w[r])
    o_ref[0] = jnp.concatenate(pieces, axis=1)


def _attn_sample(page_table, q, gt, kcmp, vcmp, ksn, vsn, kws, vws, win_first, kwn, vwn, ov, ex, kpool, vpool, past):
    n_batch, ds = q.shape[:2]
    n_pages = page_table.shape[0] // n_batch
    chunk_tokens = SAMPLE_CHUNK_PAGES * PAGE_SIZE
    per_b = lambda b, pt: (b, 0, 0)
    blk = lambda a: pl.BlockSpec((1,) + a.shape[1:], per_b)
    win = lambda a: pl.BlockSpec((NSA_KV_HEADS,) + a.shape[1:], lambda b, pt: (win_first + b, 0, 0))
    full = lambda a: pl.BlockSpec(a.shape, lambda b, pt: (0,) * a.ndim)
    slab_buf = pltpu.VMEM((2, NSA_KV_HEADS, HEAD_DIM, chunk_tokens), F32)
    grid_spec = pltpu.PrefetchScalarGridSpec(
        num_scalar_prefetch=1,
        grid=(n_batch,),
        in_specs=[blk(q), blk(gt), blk(kcmp), blk(vcmp), blk(ksn), blk(vsn), win(kws), win(vws), blk(kwn), blk(vwn),
                  full(ov), full(ex),
                  pl.BlockSpec(memory_space=pl.ANY), pl.BlockSpec(memory_space=pl.ANY)],
        out_specs=pl.BlockSpec((1, ds, NSA_WIDTH), per_b),
        scratch_shapes=[slab_buf, slab_buf, pltpu.SemaphoreType.DMA((2,)), pltpu.SemaphoreType.DMA((2,))],
    )
    return pl.pallas_call(
        functools.partial(_attn_sample_kernel, n_pages=n_pages, past=past),
        grid_spec=grid_spec,
        out_shape=jax.ShapeDtypeStruct((n_batch, ds, NSA_WIDTH), F32),
        compiler_params=pltpu.CompilerParams(dimension_semantics=("arbitrary",), vmem_limit_bytes=VMEM_LIMIT),
        name="attn_sample",
    )(page_table, q, gt, kcmp, vcmp, ksn, vsn, kws, vws, kwn, vwn, ov, ex, kpool, vpool)


CONV_ROWS = 64
FF_SPLIT = 2


def _conv_shifts(hpc, rs):
    off = hpc - (CONV_WIDTH - 1) * rs
    return sorted({(off + k * rs) % SUBLANES for k in range(CONV_WIDTH)} - {0})


def _merge_ffn_kernel(x_ref, o_ref, u_ref, cbuf_ref, fbuf_ref, cw_ref, cb_ref, lng_ref, lnb_ref, wout_ref,
                      fg_ref, wup_ref, fcw_ref, fcb_ref, wdn_ref,
                      y_ref, tail_ref, xx, xsh, cc, gg, *, rs):
    t = pl.program_id(1)
    tm = x_ref.shape[0]
    hpc = xx.shape[0] - tm
    hpf = gg.shape[0] - tm
    off_c = hpc - (CONV_WIDTH - 1) * rs
    off_f = hpf - (FFN_CONV_WIDTH - 1) * rs

    @pl.when(t == 0)
    def _():
        xx[pl.ds(0, hpc), :] = cbuf_ref[0]
        gg[pl.ds(0, hpf), :] = fbuf_ref[0]

    xx[pl.ds(hpc, tm), :] = u_ref[...]
    shifts = _conv_shifts(hpc, rs)
    span = hpc + tm - SUBLANES
    for idx, r in enumerate(shifts):
        xsh[idx, pl.ds(0, span), :] = xx[pl.ds(r, span), :]
    rc = min(CONV_ROWS, tm)
    for c0 in range(0, CONV_CH, LANES):
        for r0 in range(0, tm, rc):
            acc = jnp.zeros((rc, LANES), F32)
            for k in range(CONV_WIDTH):
                start = off_c + k * rs
                r = start % SUBLANES
                if r == 0:
                    tap = xx[pl.ds(start + r0, rc), c0:c0 + LANES]
                else:
                    tap = xsh[shifts.index(r), pl.ds(start - r + r0, rc), c0:c0 + LANES]
                acc = acc + cw_ref[k:k + 1, c0:c0 + LANES] * tap
            cc[pl.ds(r0, rc), c0:c0 + LANES] = acc
    y = cc[...] + cb_ref[...]
    mu = jnp.mean(y, -1, keepdims=True)
    var = jnp.mean(jnp.square(y - mu), -1, keepdims=True)
    y = (y - mu) * lax.rsqrt(var + EPS) * lng_ref[...] + lnb_ref[...]
    c = y * jax.nn.sigmoid(y)
    x1 = (x_ref[...] + _bdot(o_ref[...], wout_ref[0:NSA_WIDTH, :])
          + _bdot(c, wout_ref[NSA_WIDTH:NSA_WIDTH + CONV_CH, :]))
    h = (x1 * lax.rsqrt(jnp.mean(x1 * x1, -1, keepdims=True) + EPS) * fg_ref[...]).astype(BF16)
    fw = D_FF // FF_SPLIT
    f = jnp.zeros((tm, D_MODEL), F32)
    for s in range(FF_SPLIT):
        c0 = s * fw
        gg[pl.ds(hpf, tm), c0:c0 + fw] = jnp.dot(h, wup_ref[:, c0:c0 + fw], preferred_element_type=F32)
        up = jnp.dot(h, wup_ref[:, D_FF + c0:D_FF + c0 + fw], preferred_element_type=F32)
        gc = fcb_ref[:, c0:c0 + fw]
        for k in range(FFN_CONV_WIDTH):
            gc = gc + fcw_ref[k:k + 1, c0:c0 + fw] * gg[pl.ds(off_f + k * rs, tm), c0:c0 + fw]
        act = gc * jax.nn.sigmoid(gc) * up
        f = f + jnp.dot(act.astype(BF16), wdn_ref[c0:c0 + fw, :], preferred_element_type=F32)
    y_ref[...] = x1 + f
    tail_ref[0] = gg[pl.ds(tm, hpf), :]
    xx[pl.ds(0, hpc), :] = xx[pl.ds(tm, hpc), :]
    gg[pl.ds(0, hpf), :] = gg[pl.ds(tm, hpf), :]


def _merge_ffn(x, o, u, cbuf, fbuf, cw, cb, lng, lnb, wout, fg, wup, fcw, fcb, wdn, tm, rs):
    t = x.shape[0]
    n_batch = cbuf.shape[0]
    nt = t // n_batch // tm
    hpc = cbuf.shape[1]
    hpf = fbuf.shape[1]
    n_shift = len(_conv_shifts(hpc, rs))
    xsh_shape = (n_shift, hpc + tm - SUBLANES, CONV_CH) if n_shift else (1, SUBLANES, LANES)
    tok = lambda b, i: (b * nt + i, 0)
    per_b = lambda b, i: (b, 0, 0)
    return pl.pallas_call(
        functools.partial(_merge_ffn_kernel, rs=rs),
        grid=(n_batch, nt),
        in_specs=[pl.BlockSpec((tm, D_MODEL), tok), pl.BlockSpec((tm, NSA_WIDTH), tok),
                  pl.BlockSpec((tm, CONV_CH), tok),
                  pl.BlockSpec((1, hpc, CONV_CH), per_b), pl.BlockSpec((1, hpf, D_FF), per_b),
                  _const_spec(cw.shape), _const_spec(cb.shape), _const_spec(lng.shape), _const_spec(lnb.shape),
                  _const_spec(wout.shape), _const_spec(fg.shape), _const_spec(wup.shape),
                  _const_spec(fcw.shape), _const_spec(fcb.shape), _const_spec(wdn.shape)],
        out_specs=[pl.BlockSpec((tm, D_MODEL), tok), pl.BlockSpec((1, hpf, D_FF), per_b)],
        out_shape=[jax.ShapeDtypeStruct((t, D_MODEL), F32), jax.ShapeDtypeStruct((n_batch, hpf, D_FF), F32)],
        scratch_shapes=[pltpu.VMEM((hpc + tm, CONV_CH), F32), pltpu.VMEM(xsh_shape, F32),
                        pltpu.VMEM((tm, CONV_CH), F32), pltpu.VMEM((hpf + tm, D_FF), F32)],
        compiler_params=pltpu.CompilerParams(dimension_semantics=("arbitrary", "arbitrary"),
                                             vmem_limit_bytes=VMEM_LIMIT),
        name="merge_ffn",
    )(x, o, u, cbuf, fbuf, cw, cb, lng, lnb, wout, fg, wup, fcw, fcb, wdn)


def _rope_tables(pos):
    half = HEAD_DIM // 2
    inv = ROPE_THETA ** (-jnp.arange(half, dtype=F32) / half)
    ang = pos.astype(F32)[:, None] * inv[None, :]
    cos, sin = jnp.cos(ang), jnp.sin(ang)
    reps = KV_W // HEAD_DIM
    return (jnp.tile(jnp.concatenate([cos, cos], -1), (1, reps)),
            jnp.tile(jnp.concatenate([-sin, sin], -1), (1, reps)))


def _head_sum_matrix(width):
    i = jnp.arange(width) // HEAD_DIM
    return (i[:, None] == i[None, :]).astype(BF16)


def _overlap_matrix(n_cmp_rows, n_blocks, n_lanes):
    c0 = jnp.arange(n_cmp_rows) * CMP_STRIDE
    s0 = jnp.arange(n_lanes) * SEL_BLOCK
    ov = jnp.clip(jnp.minimum(c0[:, None] + CMP_BLOCK, s0[None, :] + SEL_BLOCK)
                  - jnp.maximum(c0[:, None], s0[None, :]), 0, CMP_BLOCK).astype(F32) / CMP_BLOCK
    ov = jnp.where(jnp.arange(n_lanes)[None, :] < n_blocks, ov, 0.0)
    return ov.astype(BF16)


def _expand_matrix(n_rows, n_blocks, n_keys):
    n = jnp.arange(n_rows)[:, None]
    t = jnp.arange(n_keys)[None, :] // SEL_BLOCK
    return ((n == t) & (n < n_blocks)).astype(BF16)


def _chunk_order_matrix():
    r = jnp.arange(PAGE_SIZE // 2)
    even = (r % PAGE_CHUNKS) * CMP_STRIDE + 2 * (r // PAGE_CHUNKS)
    col = jnp.arange(2 * PAGE_SIZE)[None, :]
    return ((col == even[:, None]) | (col == PAGE_SIZE + even[:, None] + 1)).astype(BF16)


def _gate_expand_matrix():
    col = jnp.arange(NSA_WIDTH)
    head = ((col % KV_W) // HEAD_DIM) * NSA_GROUP + col // KV_W
    j = jnp.arange(LANES)[None, :, None]
    br = jnp.arange(N_BRANCH)[:, None, None]
    return (j == head[None, None, :] * N_BRANCH + br).astype(BF16)


def _pair_order(a, axis):
    shape = a.shape
    a = a.reshape(shape[:axis] + (NSA_KV_HEADS, NSA_GROUP, HEAD_DIM) + shape[axis + 1:])
    return jnp.swapaxes(a, axis, axis + 1).reshape(shape)


def _slabs(a):
    t = jnp.moveaxis(a, -3, -1)
    return t.reshape((-1,) + t.shape[-2:])


def _compress_rows_weights(w1, w2, pos):
    half = CMP_STRIDE * HEAD_DIM
    eye = jnp.eye(NSA_KV_HEADS, dtype=F32)

    def spread(wpart):
        w3 = wpart.reshape(CMP_STRIDE, HEAD_DIM, CMP_HIDDEN)
        return jnp.einsum("tdh,kj->tkdjh", w3, eye).reshape(CHUNK_W, NSA_KV_HEADS * CMP_HIDDEN)

    wab = jnp.concatenate([spread(w1[:half]), spread(w1[half:])], axis=1).astype(BF16)
    w2bd = jnp.einsum("hd,kj->khjd", w2, eye).reshape(NSA_KV_HEADS * CMP_HIDDEN, KV_W).astype(BF16)

    def pos_row(p):
        row = jnp.tile(p[:, None, :], (1, NSA_KV_HEADS, 1)).reshape(1, CHUNK_W)
        return jnp.concatenate([row, jnp.zeros((SUBLANES - 1, CHUNK_W), F32)], axis=0)

    posab = jnp.concatenate([pos_row(pos[:CMP_STRIDE]), pos_row(pos[CMP_STRIDE:])], axis=0)
    return wab, posab, w2bd


def _compress_pages_weights(w1, w2, pos):
    half = CMP_STRIDE * HEAD_DIM
    wt = jnp.concatenate([w1[:half].reshape(CMP_STRIDE // 2, 2 * HEAD_DIM, CMP_HIDDEN),
                          w1[half:].reshape(CMP_STRIDE // 2, 2 * HEAD_DIM, CMP_HIDDEN)], axis=2).astype(BF16)
    pos8 = jnp.concatenate([pos.reshape(1, CMP_BLOCK * HEAD_DIM),
                            jnp.zeros((SUBLANES - 1, CMP_BLOCK * HEAD_DIM), F32)], axis=0)
    return wt, w1.astype(BF16), pos8, w2.astype(BF16)


def _tile_lanes(v, reps):
    return jnp.tile(v, reps)[None, :]


PROMPT_TM = 512
FFN_TM = 512


def kernel(x_prompt, x_sample, cache_k_cmp, cache_v_cmp, cache_k_slc, cache_v_slc, state_k_win, state_v_win,
           state_conv, state_ffn_conv, page_table, attn_norm, w_in, q_norm, k_norm, kc_norm, cmp_pos,
           cmp_w1_k, cmp_w2_k, cmp_w1_v, cmp_w2_v, conv_w, conv_b, conv_ln_g, conv_ln_b, w_out, ffn_norm,
           w_up, ffn_conv_w, ffn_conv_b, w_down):
    B, S = x_prompt.shape[:2]
    DB, DS = x_sample.shape[:2]
    depth = w_in.shape[0]
    n_pool = cache_k_cmp.shape[1]
    n_pages = page_table.shape[1]
    past = n_pages * PAGE_SIZE
    win_buf = state_k_win.shape[2]
    p_win = min(WINDOW, S)
    TS = DB * DS

    cos_p, sin_p = _rope_tables(jnp.arange(S))
    cos_s, sin_s = _rope_tables(jnp.repeat(past + jnp.arange(DS), DB))
    bdq = _head_sum_matrix(NSA_WIDTH)
    bdk = _head_sum_matrix(KV_W)
    ncp = S // CMP_STRIDE
    ncs = past // CMP_STRIDE
    nb_p = S // SEL_BLOCK
    nb_s = -(-(past + DS) // SEL_BLOCK)
    nl_s = _round_up(nb_s, LANES)
    chunk_blocks = SAMPLE_CHUNK_PAGES * PAGE_SIZE // SEL_BLOCK
    ovt_p = _overlap_matrix(ncp, nb_p, LANES).T
    oh_p = _expand_matrix(LANES, nb_p, S).T
    gx = _gate_expand_matrix()
    ov_s = _overlap_matrix(ncs, nb_s, nl_s)
    ex_s = _expand_matrix(chunk_blocks, chunk_blocks, chunk_blocks * SEL_BLOCK)
    perm = _chunk_order_matrix()
    pt_flat = page_table.reshape(-1).astype(jnp.int32)

    o_g = NSA_WIDTH + 6 * KV_W
    w_cat = jnp.concatenate([_pair_order(w_in[:, :, :NSA_WIDTH], 2), w_in[:, :, NSA_WIDTH:o_g],
                             jnp.pad(w_in[:, :, o_g:o_g + GATE_COLS], ((0, 0), (0, 0), (0, LANES - GATE_COLS))),
                             w_in[:, :, o_g + GATE_COLS:]], axis=2).astype(BF16)
    w_out_b = jnp.concatenate([_pair_order(w_out[:, :NSA_WIDTH], 1), w_out[:, NSA_WIDTH:]], axis=1).astype(BF16)
    w_up_b = w_up.astype(BF16)
    w_down_b = w_down.astype(BF16)
    conv_w_p = jnp.pad(conv_w, ((0, 0), (0, _round_up(CONV_WIDTH, SUBLANES) - CONV_WIDTH), (0, 0)))
    fcw_p = jnp.pad(ffn_conv_w, ((0, 0), (0, SUBLANES - FFN_CONV_WIDTH), (0, 0)))

    kc_pool, vc_pool, ks_pool, vs_pool = (_slabs(a) for a in (cache_k_cmp, cache_v_cmp, cache_k_slc, cache_v_slc))
    kw_states, vw_states = _slabs(state_k_win), _slabs(state_v_win)

    hpc_p = _round_up(CONV_WIDTH - 1, SUBLANES)
    hpf_p = _round_up(FFN_CONV_WIDTH - 1, SUBLANES)
    conv0 = jnp.zeros((B, hpc_p, CONV_CH), F32)
    ffn0 = jnp.zeros((B, hpf_p, D_FF), F32)

    y_p = x_prompt.reshape(B * S, D_MODEL)
    y_s = x_sample.transpose(1, 0, 2).reshape(TS, D_MODEL)

    def to_bm(a):
        return a.reshape(DS, DB, a.shape[-1]).transpose(1, 0, 2)

    def heads(a, lead):
        return a.reshape(lead + (NSA_KV_HEADS, HEAD_DIM))

    outs = [[] for _ in range(16)]
    for l in range(depth):
        qg = _tile_lanes(q_norm[l], NSA_HEADS)
        kg = jnp.concatenate([_tile_lanes(k_norm[l, i], NSA_KV_HEADS) for i in range(N_BRANCH)], axis=1)
        kcg2 = _tile_lanes(kc_norm[l], NSA_KV_HEADS)
        kcg = kc_norm[l][None]
        rw_k = _compress_rows_weights(cmp_w1_k[l], cmp_w2_k[l], cmp_pos[l])
        rw_v = _compress_rows_weights(cmp_w1_v[l], cmp_w2_v[l], cmp_pos[l])
        pw_k = _compress_pages_weights(cmp_w1_k[l], cmp_w2_k[l], cmp_pos[l])
        pw_v = _compress_pages_weights(cmp_w1_v[l], cmp_w2_v[l], cmp_pos[l])
        merge_w = (conv_w_p[l], conv_b[l][None], conv_ln_g[l][None], conv_ln_b[l][None], w_out_b[l],
                   ffn_norm[l][None], w_up_b[l], fcw_p[l], ffn_conv_b[l][None], w_down_b[l])
        pt_l = pt_flat + l * n_pool

        q, kc, vc, ks, vs, kw, vw, gt, u = _inproj(y_p, attn_norm[l][None], w_cat[l], qg, kg, cos_p, sin_p, bdq,
                                                   PROMPT_TM)
        kcmp = _compress_rows(kc, *rw_k, kcg2, bdk, B, True)
        vcmp = _compress_rows(vc, *rw_v, kcg2, bdk, B, False)
        o = _attn_prompt(q, gt, kcmp, vcmp, ks, vs, kw, vw, ovt_p, oh_p, gx, B, S)
        y_p, tail = _merge_ffn(y_p, o, u, conv0, ffn0, *merge_w, FFN_TM, 1)
        for idx, a in enumerate((kc, vc, ks, vs)):
            outs[idx].append(heads(a, (B, S)))
        outs[4].append(heads(kw, (B, S))[:, -p_win:])
        outs[5].append(heads(vw, (B, S))[:, -p_win:])
        outs[6].append(u.reshape(B, S, CONV_CH)[:, -(CONV_WIDTH - 1):])
        outs[7].append(tail[:, -(FFN_CONV_WIDTH - 1):])

        q, kc, vc, ks, vs, kw, vw, gt, u = _inproj(y_s, attn_norm[l][None], w_cat[l], qg, kg, cos_s, sin_s, bdq, TS)
        kcmp = _compress_pages(pt_l, kc_pool, perm, *pw_k, kcg, DB, True)
        vcmp = _compress_pages(pt_l, vc_pool, perm, *pw_v, kcg, DB, False)
        kc, vc, ks, vs, kw, vw, u_bm = (to_bm(a) for a in (kc, vc, ks, vs, kw, vw, u))
        o = _attn_sample(pt_l, to_bm(q), to_bm(gt), kcmp, vcmp, ks, vs, kw_states, vw_states, l * DB, kw, vw,
                         ov_s, ex_s, ks_pool, vs_pool, past)
        o_tm = o.transpose(1, 0, 2).reshape(TS, NSA_WIDTH)
        cbuf = state_conv[l].transpose(1, 0, 2).reshape(1, (CONV_WIDTH - 1) * DB, CONV_CH)
        fbuf = state_ffn_conv[l].transpose(1, 0, 2).reshape(1, (FFN_CONV_WIDTH - 1) * DB, D_FF)
        y_s, tail = _merge_ffn(y_s, o_tm, u, cbuf, fbuf, *merge_w, TS, DB)
        for idx, a in enumerate((kc, vc, ks, vs)):
            outs[8 + idx].append(heads(a, (DB, DS)))
        outs[12].append(jnp.concatenate([state_k_win[l], heads(kw, (DB, DS))], 1)[:, -win_buf:])
        outs[13].append(jnp.concatenate([state_v_win[l], heads(vw, (DB, DS))], 1)[:, -win_buf:])
        outs[14].append(jnp.concatenate([state_conv[l], u_bm], 1)[:, -(CONV_WIDTH - 1):])
        outs[15].append(tail.reshape(FFN_CONV_WIDTH - 1, DB, D_FF).transpose(1, 0, 2))

    y_prompt = y_p.reshape(B, S, D_MODEL)
    y_sample = y_s.reshape(DS, DB, D_MODEL).transpose(1, 0, 2)
    return (y_prompt, y_sample) + tuple(jnp.stack(o) for o in outs)
```

```python
import functools

import jax
import jax.numpy as jnp
from jax import lax
from jax.experimental import pallas as pl
from jax.experimental.pallas import tpu as pltpu

F32 = jnp.float32
BF16 = jnp.bfloat16

D_MODEL = 1024
HEAD_DIM = 64
NSA_HEADS = 8
NSA_KV_HEADS = 2
NSA_GROUP = NSA_HEADS // NSA_KV_HEADS
NSA_WIDTH = NSA_HEADS * HEAD_DIM
KV_W = NSA_KV_HEADS * HEAD_DIM
CONV_CH = D_MODEL - NSA_WIDTH
N_BRANCH = 3
CMP_BLOCK = 32
CMP_STRIDE = 16
CMP_HIDDEN = 128
SEL_BLOCK = 64
N_SELECT = 16
WINDOW = 512
CONV_WIDTH = 31
D_FF = 2816
FFN_CONV_WIDTH = 3
ROPE_THETA = 10000.0
EPS = 1e-6
Q_BLOCK = 128
GATE_COLS = NSA_HEADS * N_BRANCH
PAGE_SIZE = 128
SCALE = HEAD_DIM ** -0.5
NEG = -1e30

LANES = 128
SUBLANES = 8
VMEM_LIMIT = 52 * 1024 * 1024
CHUNK_W = CMP_STRIDE * KV_W
PAGE_CHUNKS = PAGE_SIZE // CMP_STRIDE
Z_COLS = NSA_WIDTH + 6 * KV_W + LANES + 2 * CONV_CH
NT_DIMS = (((1,), (1,)), ((), ()))

assert NSA_KV_HEADS == 2 and KV_W == LANES and PAGE_SIZE == LANES and CMP_BLOCK == 2 * CMP_STRIDE


def _round_up(x, m):
    return (x + m - 1) // m * m


def _bdot(a, b):
    return jnp.dot(a.astype(BF16), b.astype(BF16), preferred_element_type=F32)


def _bdot_nt(a, b):
    return lax.dot_general(a.astype(BF16), b.astype(BF16), NT_DIMS, preferred_element_type=F32)


def _split(a):
    hi = a.astype(BF16)
    return hi, (a - hi.astype(F32)).astype(BF16)


def _split_dot(a, b_bf16):
    hi, lo = _split(a)
    return (jnp.dot(hi, b_bf16, preferred_element_type=F32)
            + jnp.dot(lo, b_bf16, preferred_element_type=F32))


def _head_rms(x, bd, gain):
    ms = _split_dot(x * x, bd) * (1.0 / HEAD_DIM)
    return x * lax.rsqrt(ms + EPS) * gain


def _rope(y, cos, sin_signed):
    w = y.shape[-1]
    half = HEAD_DIM // 2
    fwd = pltpu.roll(y, w - half, 1)
    bwd = pltpu.roll(y, half, 1)
    lane = lax.broadcasted_iota(jnp.int32, y.shape, 1)
    swapped = jnp.where((lane & (HEAD_DIM - 1)) < half, fwd, bwd)
    return y * cos + swapped * sin_signed


def _masked_softmax(s, mask):
    s = jnp.where(mask, s, NEG)
    m = jnp.max(s, -1, keepdims=True)
    e = jnp.where(mask, jnp.exp(s - m), 0.0)
    return e / jnp.maximum(jnp.sum(e, -1, keepdims=True), 1e-30)


def _softmax_start(rows, width):
    return jnp.full((rows, 1), NEG, F32), jnp.zeros((rows, 1), F32), jnp.zeros((rows, width), F32)


def _softmax_update(state, s, mask, weigh):
    m_i, l_i, acc = state
    s = jnp.where(mask, s, NEG)
    m_new = jnp.maximum(m_i, jnp.max(s, -1, keepdims=True))
    alpha = jnp.exp(m_i - m_new)
    e = jnp.where(mask, jnp.exp(s - m_new), 0.0)
    return m_new, alpha * l_i + jnp.sum(e, -1, keepdims=True), alpha * acc + weigh(e)


def _softmax_finish(state):
    _, l_i, acc = state
    return acc / jnp.maximum(l_i, 1e-30)


def _const_spec(shape):
    nd = len(shape)
    return pl.BlockSpec(shape, lambda *_: (0,) * nd, pipeline_mode=pl.Buffered(1))


def _inproj_kernel(x_ref, g_ref, w_ref, qg_ref, kg_ref, cos_ref, sin_ref, bd_ref,
                   q_ref, kc_ref, vc_ref, ks_ref, vs_ref, kw_ref, vw_ref, gt_ref, u_ref):
    x = x_ref[...]
    h = x * lax.rsqrt(jnp.mean(x * x, -1, keepdims=True) + EPS) * g_ref[...]
    z = jnp.dot(h.astype(BF16), w_ref[...], preferred_element_type=F32)
    cos = cos_ref[...]
    sin = sin_ref[...]
    bdq = bd_ref[...]
    bdk = bdq[:KV_W, :KV_W]
    nrep = NSA_WIDTH // KV_W
    q = _head_rms(z[:, :NSA_WIDTH], bdq, qg_ref[...])
    q_ref[...] = _rope(q, jnp.concatenate([cos] * nrep, axis=1), jnp.concatenate([sin] * nrep, axis=1))
    kg = kg_ref[...]
    o = NSA_WIDTH
    kc_ref[...] = _rope(_head_rms(z[:, o:o + KV_W], bdk, kg[:, 0:KV_W]), cos, sin)
    vc_ref[...] = z[:, o + KV_W:o + 2 * KV_W]
    ks_ref[...] = _rope(_head_rms(z[:, o + 2 * KV_W:o + 3 * KV_W], bdk, kg[:, KV_W:2 * KV_W]), cos, sin)
    vs_ref[...] = z[:, o + 3 * KV_W:o + 4 * KV_W]
    kw_ref[...] = _rope(_head_rms(z[:, o + 4 * KV_W:o + 5 * KV_W], bdk, kg[:, 2 * KV_W:3 * KV_W]), cos, sin)
    vw_ref[...] = z[:, o + 5 * KV_W:o + 6 * KV_W]
    o += 6 * KV_W
    gt_ref[...] = jax.nn.sigmoid(z[:, o:o + LANES])
    o += LANES
    u_ref[...] = z[:, o:o + CONV_CH] * jax.nn.sigmoid(z[:, o + CONV_CH:o + 2 * CONV_CH])


def _inproj(x, g, w, qg, kg, cos, sin, bd, tm):
    t = x.shape[0]
    nt = t // tm
    npos = cos.shape[0] // tm
    row = lambda i: (i, 0)
    pos = lambda i: (i % npos, 0)
    widths = [NSA_WIDTH] + [KV_W] * 6 + [LANES, CONV_CH]
    return pl.pallas_call(
        _inproj_kernel,
        grid=(nt,),
        in_specs=[pl.BlockSpec((tm, D_MODEL), row), _const_spec((1, D_MODEL)), _const_spec((D_MODEL, Z_COLS)),
                  _const_spec((1, NSA_WIDTH)), _const_spec((1, 3 * KV_W)),
                  pl.BlockSpec((tm, KV_W), pos), pl.BlockSpec((tm, KV_W), pos),
                  _const_spec((NSA_WIDTH, NSA_WIDTH))],
        out_specs=[pl.BlockSpec((tm, wd), row) for wd in widths],
        out_shape=[jax.ShapeDtypeStruct((t, wd), F32) for wd in widths],
        compiler_params=pltpu.CompilerParams(dimension_semantics=("arbitrary",), vmem_limit_bytes=VMEM_LIMIT),
        name="inproj",
    )(x, g, w, qg, kg, cos, sin, bd)


def _compress_rows_kernel(x_ref, wab_ref, pos_ref, w2_ref, g_ref, bd_ref, out_ref, *, norm):
    x = x_ref[0]
    nch = x.shape[0]
    hid2 = NSA_KV_HEADS * CMP_HIDDEN
    wab = wab_ref[...]
    z = jnp.dot(x.astype(BF16), wab, preferred_element_type=F32)
    pos = pos_ref[...].astype(BF16)
    bias = (jnp.dot(pos[0:SUBLANES], wab[:, :hid2], preferred_element_type=F32)
            + jnp.dot(pos[SUBLANES:], wab[:, hid2:], preferred_element_type=F32))[0:1]
    nxt = pltpu.roll(z[:, hid2:], nch - 1, 0)
    hid = jax.nn.gelu(z[:, :hid2] + nxt + bias)
    y = jnp.dot(hid.astype(BF16), w2_ref[...], preferred_element_type=F32)
    if norm:
        y = _head_rms(y, bd_ref[...], g_ref[...])
    rowi = lax.broadcasted_iota(jnp.int32, y.shape, 0)
    out_ref[0] = jnp.where(rowi < nch - 1, y, 0.0)


def _compress_rows(rows, wab, pos, w2, g, bd, n_batch, norm):
    x = rows.reshape(n_batch, -1, CHUNK_W)
    nch = x.shape[1]
    return pl.pallas_call(
        functools.partial(_compress_rows_kernel, norm=norm),
        grid=(n_batch,),
        in_specs=[pl.BlockSpec((1, nch, CHUNK_W), lambda b: (b, 0, 0)), _const_spec(wab.shape),
                  _const_spec(pos.shape), _const_spec(w2.shape), _const_spec(g.shape), _const_spec(bd.shape)],
        out_specs=pl.BlockSpec((1, nch, KV_W), lambda b: (b, 0, 0)),
        out_shape=jax.ShapeDtypeStruct((n_batch, nch, KV_W), F32),
        compiler_params=pltpu.CompilerParams(dimension_semantics=("arbitrary",), vmem_limit_bytes=VMEM_LIMIT),
        name="compress_rows",
    )(x, wab, pos, w2, g, bd)


def _page_copies(pt_ref, pool_ref, dst, sem, first, n_pages):
    copies = []
    for p in range(n_pages):
        page = pt_ref[first + p]
        for h in range(NSA_KV_HEADS):
            copies.append(pltpu.make_async_copy(pool_ref.at[page * NSA_KV_HEADS + h],
                                                dst.at[h, :, pl.ds(p * PAGE_SIZE, PAGE_SIZE)], sem))
    return copies


REGROUP_UNROLL = 16


def _compress_pages_kernel(pt_ref, pool_ref, perm_ref, wt_ref, w1_ref, pos_ref, w2_ref, g_ref, out_ref,
                           buf, xg, sem, *, n_pages, norm):
    b = pl.program_id(0)
    nb = pl.num_programs(0)

    def copies(batch, slot):
        return _page_copies(pt_ref, pool_ref, buf.at[slot], sem.at[slot], batch * n_pages, n_pages)

    @pl.when(b == 0)
    def _():
        for c in copies(0, 0):
            c.start()

    @pl.when(b + 1 < nb)
    def _():
        for c in copies(b + 1, (b + 1) % 2):
            c.start()

    slot = b % 2
    for c in copies(b, slot):
        c.wait()

    perm = perm_ref[...]

    def regroup(i, carry):
        for k in range(REGROUP_UNROLL):
            p = i * REGROUP_UNROLL + k
            t0 = pl.multiple_of(p * PAGE_SIZE, PAGE_SIZE)
            for h in range(NSA_KV_HEADS):
                slab = buf[slot, h, :, pl.ds(t0, PAGE_SIZE)].astype(BF16)
                xg[h, p] = lax.dot_general(perm, slab, NT_DIMS, preferred_element_type=F32)
        return carry

    lax.fori_loop(0, n_pages // REGROUP_UNROLL, regroup, 0)

    nch = n_pages * PAGE_CHUNKS
    acc = jnp.zeros((NSA_KV_HEADS * nch, 2 * CMP_HIDDEN), F32)
    for t in range(CMP_STRIDE):
        x = jnp.concatenate([xg[h, :, pl.ds(t * PAGE_CHUNKS, PAGE_CHUNKS), :].reshape(nch, HEAD_DIM)
                             for h in range(NSA_KV_HEADS)], axis=0)
        acc = acc + jnp.dot(x.astype(BF16), wt_ref[t], preferred_element_type=F32)
    bias = jnp.dot(pos_ref[...].astype(BF16), w1_ref[...], preferred_element_type=F32)[0:1]
    nxt = pltpu.roll(acc[:, CMP_HIDDEN:], NSA_KV_HEADS * nch - 1, 0)
    hid = jax.nn.gelu(acc[:, :CMP_HIDDEN] + nxt + bias)
    y = jnp.dot(hid.astype(BF16), w2_ref[...], preferred_element_type=F32)
    if norm:
        y = y * lax.rsqrt(jnp.mean(y * y, -1, keepdims=True) + EPS) * g_ref[...]
    rowi = lax.broadcasted_iota(jnp.int32, y.shape, 0)
    last = (rowi == nch - 1) | (rowi == NSA_KV_HEADS * nch - 1)
    out_ref[0] = jnp.where(last, 0.0, y)


def _compress_pages(page_table, pool, perm, wt, w1, pos, w2, g, n_batch, norm):
    n_pages = page_table.shape[0] // n_batch
    nch = n_pages * PAGE_CHUNKS
    full = lambda a: pl.BlockSpec(a.shape, lambda b, pt: (0,) * a.ndim)
    grid_spec = pltpu.PrefetchScalarGridSpec(
        num_scalar_prefetch=1,
        grid=(n_batch,),
        in_specs=[pl.BlockSpec(memory_space=pl.ANY), full(perm), full(wt), full(w1), full(pos), full(w2), full(g)],
        out_specs=pl.BlockSpec((1, NSA_KV_HEADS * nch, HEAD_DIM), lambda b, pt: (b, 0, 0)),
        scratch_shapes=[pltpu.VMEM((2, NSA_KV_HEADS, HEAD_DIM, n_pages * PAGE_SIZE), F32),
                        pltpu.VMEM((NSA_KV_HEADS, n_pages, PAGE_SIZE, HEAD_DIM), F32),
                        pltpu.SemaphoreType.DMA((2,))],
    )
    return pl.pallas_call(
        functools.partial(_compress_pages_kernel, n_pages=n_pages, norm=norm),
        grid_spec=grid_spec,
        out_shape=jax.ShapeDtypeStruct((n_batch, NSA_KV_HEADS * nch, HEAD_DIM), F32),
        compiler_params=pltpu.CompilerParams(dimension_semantics=("arbitrary",), vmem_limit_bytes=VMEM_LIMIT),
        name="compress_pages",
    )(page_table, pool, perm, wt, w1, pos, w2, g)


def _select_lanes(imp, cur, n_blocks):
    n = lax.broadcasted_iota(jnp.int32, (1, imp.shape[1]), 1)
    forced = (n == 0) | (n == cur) | (n == cur - 1)
    v = jnp.where(forced, jnp.inf, jnp.where(n <= cur, imp, -jnp.inf))
    rank = jnp.zeros(imp.shape, jnp.int32)
    for m in range(n_blocks):
        vm = v[:, m:m + 1]
        before = (n > m).astype(jnp.int32)
        rank = rank + jnp.where(vm > v, 1, jnp.where(vm == v, before, 0))
    return jnp.where(rank < min(N_SELECT, n_blocks), 1.0, 0.0).astype(BF16)


def _select_rows(imp_t, cur, n_blocks):
    n = lax.broadcasted_iota(jnp.int32, imp_t.shape, 0)
    forced = (n == 0) | (n == cur) | (n == cur - 1)
    v = jnp.where(forced, jnp.inf, jnp.where(n <= cur, imp_t, -jnp.inf))
    rank = jnp.zeros(imp_t.shape, jnp.int32)
    for m in range(n_blocks):
        vm = v[m:m + 1, :]
        rank = rank + jnp.where(vm > v, 1, jnp.where(vm == v, (n > m).astype(jnp.int32), 0))
    return jnp.where(rank < min(N_SELECT, n_blocks), 1.0, 0.0)


KEY_BLOCK = 512


def _attn_prompt_kernel(q_ref, gt_ref, kcmp_ref, vcmp_ref, ks_ref, vs_ref, kw_ref, vw_ref, ovt_ref, oh_ref, gx_ref,
                        o_ref, ksa, vsb, kwb, vwb, kcb, vcb, *, seq):
    i = pl.program_id(1)

    @pl.when(i == 0)
    def _():
        ksa[:, 0:KV_W] = ks_ref[0].astype(BF16)
        ksa[:, KV_W:2 * KV_W] = oh_ref[...]
        vsb[...] = vs_ref[0].astype(BF16)
        kwb[...] = kw_ref[0].astype(BF16)
        vwb[...] = vw_ref[0].astype(BF16)
        kcb[...] = kcmp_ref[0].astype(BF16)
        vcb[...] = vcmp_ref[0].astype(BF16)

    qs = i * Q_BLOCK
    q = q_ref[...] * SCALE
    rows = NSA_GROUP * Q_BLOCK
    lane = lax.broadcasted_iota(jnp.int32, (1, KV_W), 1)
    qpos4 = qs + (lax.broadcasted_iota(jnp.int32, (rows, 1), 0) & (Q_BLOCK - 1))
    qpos_l = qs + lax.broadcasted_iota(jnp.int32, (1, Q_BLOCK), 1)
    n_cmp = kcb.shape[0]
    n_blocks = seq // SEL_BLOCK
    win_keys = WINDOW + Q_BLOCK
    n_kb = (qs + Q_BLOCK + KEY_BLOCK - 1) // KEY_BLOCK
    w0 = pl.multiple_of(jnp.maximum(qs - WINDOW, 0), Q_BLOCK)
    o_c, o_w, q4a = [], [], []
    for h in range(NSA_KV_HEADS):
        in_head = (lane >= h * HEAD_DIM) & (lane < (h + 1) * HEAD_DIM)
        q4 = jnp.concatenate([jnp.where(in_head, q[:, g * KV_W:(g + 1) * KV_W], 0.0) for g in range(NSA_GROUP)],
                             axis=0).astype(BF16)
        s_c = lax.dot_general(q4, kcb[...], NT_DIMS, preferred_element_type=F32)
        c = lax.broadcasted_iota(jnp.int32, (1, n_cmp), 1)
        p_c = _masked_softmax(s_c, (c * CMP_STRIDE + (CMP_BLOCK - 1)) <= qpos4)
        o_c.append(jnp.dot(p_c.astype(BF16), vcb[...], preferred_element_type=F32))
        p_sum = p_c[0:Q_BLOCK]
        for g in range(1, NSA_GROUP):
            p_sum = p_sum + p_c[g * Q_BLOCK:(g + 1) * Q_BLOCK]
        hi, lo = _split(p_sum)
        ovt = ovt_ref[...]
        imp_t = (lax.dot_general(ovt, hi, NT_DIMS, preferred_element_type=F32)
                 + lax.dot_general(ovt, lo, NT_DIMS, preferred_element_type=F32))[0:n_blocks]
        sel_t = _select_rows(imp_t, qpos_l // SEL_BLOCK, n_blocks)
        bias_t = jnp.concatenate([jnp.where(sel_t > 0.5, 0.0, NEG),
                                  jnp.zeros((KV_W - n_blocks, Q_BLOCK), F32)], axis=0)
        bias = bias_t.T.astype(BF16)
        q4a.append(jnp.concatenate([q4, jnp.concatenate([bias] * NSA_GROUP, axis=0)], axis=1))
        s_w = lax.dot_general(q4, kwb[pl.ds(w0, win_keys), :], NT_DIMS, preferred_element_type=F32)
        dp = qpos4 - (w0 + lax.broadcasted_iota(jnp.int32, (1, win_keys), 1))
        s_w = jnp.where((dp >= 0) & (dp < WINDOW), s_w, NEG)
        e_w = jnp.exp(s_w - jnp.max(s_w, -1, keepdims=True))
        acc_w = jnp.dot(e_w.astype(BF16), vwb[pl.ds(w0, win_keys), :], preferred_element_type=F32)
        o_w.append(acc_w * (1.0 / jnp.sum(e_w, -1, keepdims=True)))

    def sel_block(j, carry, causal):
        k0 = pl.multiple_of(j * KEY_BLOCK, KEY_BLOCK)
        kb = ksa[pl.ds(k0, KEY_BLOCK), :]
        vb = vsb[pl.ds(k0, KEY_BLOCK), :]
        new = []
        for h in range(NSA_KV_HEADS):
            m_i, l_i, acc = carry[h]
            s = lax.dot_general(q4a[h], kb, NT_DIMS, preferred_element_type=F32)
            if causal:
                tok = k0 + lax.broadcasted_iota(jnp.int32, (1, KEY_BLOCK), 1)
                s = jnp.where(tok <= qpos4, s, NEG)
            m_new = jnp.maximum(m_i, jnp.max(s, -1, keepdims=True))
            alpha = jnp.exp(m_i - m_new)
            e = jnp.exp(s - m_new)
            l_new = alpha * l_i + jnp.sum(e, -1, keepdims=True)
            new.append((m_new, l_new, alpha * acc + jnp.dot(e.astype(BF16), vb, preferred_element_type=F32)))
        return tuple(new)

    carry = tuple(_softmax_start(rows, KV_W) for _ in range(NSA_KV_HEADS))
    carry = lax.fori_loop(0, n_kb - 1, functools.partial(sel_block, causal=False), carry)
    carry = sel_block(n_kb - 1, carry, True)
    o_s = [acc_s * (1.0 / l_s) for _, l_s, acc_s in carry]

    first = lane < HEAD_DIM
    ghi, glo = _split(gt_ref[...])
    out = jnp.zeros((Q_BLOCK, NSA_WIDTH), F32)
    for br, parts in enumerate((o_c, o_s, o_w)):
        gx = gx_ref[br]
        gate = jnp.dot(ghi, gx, preferred_element_type=F32) + jnp.dot(glo, gx, preferred_element_type=F32)
        merged = jnp.concatenate([jnp.where(first, parts[0][g * Q_BLOCK:(g + 1) * Q_BLOCK],
                                            parts[1][g * Q_BLOCK:(g + 1) * Q_BLOCK])
                                  for g in range(NSA_GROUP)], axis=1)
        out = out + gate * merged
    o_ref[...] = out


def _attn_prompt(q, gt, kcmp, vcmp, ks, vs, kw, vw, ovt, oh, gx, n_batch, seq):
    nq = seq // Q_BLOCK
    n_cmp = kcmp.shape[1]
    tok = lambda b, i: (b * nq + i, 0)
    per_b = lambda b, i: (b, 0, 0)
    kv3 = lambda a: a.reshape(n_batch, seq, KV_W)
    return pl.pallas_call(
        functools.partial(_attn_prompt_kernel, seq=seq),
        grid=(n_batch, nq),
        in_specs=[pl.BlockSpec((Q_BLOCK, NSA_WIDTH), tok), pl.BlockSpec((Q_BLOCK, LANES), tok),
                  pl.BlockSpec((1, n_cmp, KV_W), per_b), pl.BlockSpec((1, n_cmp, KV_W), per_b),
                  pl.BlockSpec((1, seq, KV_W), per_b), pl.BlockSpec((1, seq, KV_W), per_b),
                  pl.BlockSpec((1, seq, KV_W), per_b), pl.BlockSpec((1, seq, KV_W), per_b),
                  _const_spec(ovt.shape), _const_spec(oh.shape), _const_spec(gx.shape)],
        out_specs=pl.BlockSpec((Q_BLOCK, NSA_WIDTH), tok),
        out_shape=jax.ShapeDtypeStruct((n_batch * seq, NSA_WIDTH), F32),
        scratch_shapes=[pltpu.VMEM((seq, 2 * KV_W), BF16), pltpu.VMEM((seq, KV_W), BF16),
                        pltpu.VMEM((seq, KV_W), BF16), pltpu.VMEM((seq, KV_W), BF16),
                        pltpu.VMEM((n_cmp, KV_W), BF16), pltpu.VMEM((n_cmp, KV_W), BF16)],
        compiler_params=pltpu.CompilerParams(dimension_semantics=("arbitrary", "arbitrary"),
                                             vmem_limit_bytes=VMEM_LIMIT),
        name="attn_prompt",
    )(q, gt, kcmp, vcmp, kv3(ks), kv3(vs), kv3(kw), kv3(vw), ovt, oh, gx)


SAMPLE_CHUNK_PAGES = 16


def _attn_sample_kernel(pt_ref, q_ref, gt_ref, kcmp_ref, vcmp_ref, ksn_ref, vsn_ref, kws_ref, vws_ref,
                        kwn_ref, vwn_ref, ov_ref, ex_ref, kpool_ref, vpool_ref,
                        o_ref, kbuf, vbuf, ksem, vsem, *, n_pages, past):
    b = pl.program_id(0)
    nb = pl.num_programs(0)
    n_chunks = n_pages // SAMPLE_CHUNK_PAGES
    chunk_blocks = SAMPLE_CHUNK_PAGES * PAGE_SIZE // SEL_BLOCK
    assert n_chunks % 2 == 0

    def copies(batch, chunk):
        slot = chunk % 2
        first = batch * n_pages + chunk * SAMPLE_CHUNK_PAGES
        return (_page_copies(pt_ref, kpool_ref, kbuf.at[slot], ksem.at[slot], first, SAMPLE_CHUNK_PAGES)
                + _page_copies(pt_ref, vpool_ref, vbuf.at[slot], vsem.at[slot], first, SAMPLE_CHUNK_PAGES))

    @pl.when(b == 0)
    def _():
        for c in copies(0, 0):
            c.start()

    ds = q_ref.shape[1]
    q = q_ref[0] * SCALE
    gt = gt_ref[0]
    per_head = NSA_GROUP * ds
    heads = range(NSA_KV_HEADS)
    hs = [slice(h * HEAD_DIM, (h + 1) * HEAD_DIM) for h in heads]
    qh = [jnp.concatenate([q[:, g * KV_W + h * HEAD_DIM:g * KV_W + (h + 1) * HEAD_DIM] for g in range(NSA_GROUP)],
                          axis=0).astype(BF16) for h in heads]
    rows = NSA_KV_HEADS * per_head
    trow = lax.broadcasted_iota(jnp.int32, (rows, 1), 0) & (ds - 1)
    qpos = past + trow

    def scores_t(kt):
        return jnp.concatenate([jnp.dot(qh[h], kt[h].astype(BF16), preferred_element_type=F32) for h in heads],
                               axis=0)

    def scores(k):
        return jnp.concatenate([_bdot_nt(qh[h], k[h]) for h in heads], axis=0)

    def weigh_t(vt):
        return lambda e: jnp.concatenate([_bdot_nt(e[h * per_head:(h + 1) * per_head], vt[h]) for h in heads],
                                         axis=0)

    def weigh(v):
        return lambda e: jnp.concatenate([_bdot(e[h * per_head:(h + 1) * per_head], v[h]) for h in heads], axis=0)

    n_cmp = kcmp_ref.shape[1] // NSA_KV_HEADS
    kc = kcmp_ref[0]
    vc = vcmp_ref[0]
    cmp_rows = [slice(h * n_cmp, (h + 1) * n_cmp) for h in heads]
    c = lax.broadcasted_iota(jnp.int32, (1, n_cmp), 1)
    p_c = _masked_softmax(scores([kc[r] for r in cmp_rows]), (c * CMP_STRIDE + (CMP_BLOCK - 1)) <= qpos)
    o_c = weigh([vc[r] for r in cmp_rows])(p_c)
    sums = []
    for h in heads:
        acc = p_c[h * per_head:h * per_head + ds]
        for g in range(1, NSA_GROUP):
            acc = acc + p_c[h * per_head + g * ds:h * per_head + (g + 1) * ds]
        sums.append(acc)
    imp = _split_dot(jnp.concatenate(sums, axis=0), ov_ref[...])
    qpos_sel = past + (lax.broadcasted_iota(jnp.int32, (NSA_KV_HEADS * ds, 1), 0) & (ds - 1))
    n_blocks = -(-(past + ds) // SEL_BLOCK)
    sel = _select_lanes(imp, qpos_sel // SEL_BLOCK, n_blocks)
    sel = jnp.concatenate([sel[h * ds:(h + 1) * ds] for h in heads for _ in range(NSA_GROUP)], axis=0)

    state = _softmax_start(rows, HEAD_DIM)
    for ch in range(n_chunks):
        if ch + 1 < n_chunks:
            for cp in copies(b, ch + 1):
                cp.start()
        else:
            @pl.when(b + 1 < nb)
            def _():
                for cp in copies(b + 1, 0):
                    cp.start()
        for cp in copies(b, ch):
            cp.wait()
        slot = ch % 2
        selm = jnp.dot(sel[:, ch * chunk_blocks:(ch + 1) * chunk_blocks], ex_ref[...], preferred_element_type=F32)
        state = _softmax_update(state, scores_t([kbuf[slot, h] for h in heads]), selm > 0.5,
                                weigh_t([vbuf[slot, h] for h in heads]))
    new_block = past // SEL_BLOCK
    tok = lax.broadcasted_iota(jnp.int32, (1, ds), 1)
    picked = sel[:, new_block:new_block + 1].astype(F32) > 0.5
    ksn = ksn_ref[0]
    vsn = vsn_ref[0]
    state = _softmax_update(state, scores([ksn[:, s] for s in hs]), picked & (tok <= trow),
                            weigh([vsn[:, s] for s in hs]))
    o_s = _softmax_finish(state)

    nwin = kws_ref.shape[2]
    state = _softmax_start(rows, HEAD_DIM)
    dp = qpos - (past - nwin + lax.broadcasted_iota(jnp.int32, (1, nwin), 1))
    state = _softmax_update(state, scores_t([kws_ref[h] for h in heads]), (dp >= 0) & (dp < WINDOW),
                            weigh_t([vws_ref[h] for h in heads]))
    dp = trow - tok
    kwn = kwn_ref[0]
    vwn = vwn_ref[0]
    state = _softmax_update(state, scores([kwn[:, s] for s in hs]), (dp >= 0) & (dp < WINDOW),
                            weigh([vwn[:, s] for s in hs]))
    o_w = _softmax_finish(state)

    pieces = []
    for g in range(NSA_GROUP):
        for h in heads:
            j = (h * NSA_GROUP + g) * N_BRANCH
            r = slice(h * per_head + g * ds, h * per_head + (g + 1) * ds)
            pieces.append(gt[:, j:j + 1] * o_c[r] + gt[:, j + 1:j + 2] * o_s[r] + gt[:, j + 2:j + 3] * o_w[r])
    o_ref[0] = jnp.concatenate(pieces, axis=1)


def _attn_sample(page_table, q, gt, kcmp, vcmp, ksn, vsn, kws, vws, win_first, kwn, vwn, ov, ex, kpool, vpool, past):
    n_batch, ds = q.shape[:2]
    n_pages = page_table.shape[0] // n_batch
    chunk_tokens = SAMPLE_CHUNK_PAGES * PAGE_SIZE
    per_b = lambda b, pt: (b, 0, 0)
    blk = lambda a: pl.BlockSpec((1,) + a.shape[1:], per_b)
    win = lambda a: pl.BlockSpec((NSA_KV_HEADS,) + a.shape[1:], lambda b, pt: (win_first + b, 0, 0))
    full = lambda a: pl.BlockSpec(a.shape, lambda b, pt: (0,) * a.ndim)
    slab_buf = pltpu.VMEM((2, NSA_KV_HEADS, HEAD_DIM, chunk_tokens), F32)
    grid_spec = pltpu.PrefetchScalarGridSpec(
        num_scalar_prefetch=1,
        grid=(n_batch,),
        in_specs=[blk(q), blk(gt), blk(kcmp), blk(vcmp), blk(ksn), blk(vsn), win(kws), win(vws), blk(kwn), blk(vwn),
                  full(ov), full(ex),
                  pl.BlockSpec(memory_space=pl.ANY), pl.BlockSpec(memory_space=pl.ANY)],
        out_specs=pl.BlockSpec((1, ds, NSA_WIDTH), per_b),
        scratch_shapes=[slab_buf, slab_buf, pltpu.SemaphoreType.DMA((2,)), pltpu.SemaphoreType.DMA((2,))],
    )
    return pl.pallas_call(
        functools.partial(_attn_sample_kernel, n_pages=n_pages, past=past),
        grid_spec=grid_spec,
        out_shape=jax.ShapeDtypeStruct((n_batch, ds, NSA_WIDTH), F32),
        compiler_params=pltpu.CompilerParams(dimension_semantics=("arbitrary",), vmem_limit_bytes=VMEM_LIMIT),
        name="attn_sample",
    )(page_table, q, gt, kcmp, vcmp, ksn, vsn, kws, vws, kwn, vwn, ov, ex, kpool, vpool)


CONV_ROWS = 64
FF_SPLIT = 2


def _conv_shifts(hpc, rs):
    off = hpc - (CONV_WIDTH - 1) * rs
    return sorted({(off + k * rs) % SUBLANES for k in range(CONV_WIDTH)} - {0})


def _merge_ffn_kernel(x_ref, o_ref, u_ref, cbuf_ref, fbuf_ref, cw_ref, cb_ref, lng_ref, lnb_ref, wout_ref,
                      fg_ref, wup_ref, fcw_ref, fcb_ref, wdn_ref,
                      y_ref, tail_ref, xx, xsh, cc, gg, *, rs):
    t = pl.program_id(1)
    tm = x_ref.shape[0]
    hpc = xx.shape[0] - tm
    hpf = gg.shape[0] - tm
    off_c = hpc - (CONV_WIDTH - 1) * rs
    off_f = hpf - (FFN_CONV_WIDTH - 1) * rs

    @pl.when(t == 0)
    def _():
        xx[pl.ds(0, hpc), :] = cbuf_ref[0]
        gg[pl.ds(0, hpf), :] = fbuf_ref[0]

    xx[pl.ds(hpc, tm), :] = u_ref[...]
    shifts = _conv_shifts(hpc, rs)
    span = hpc + tm - SUBLANES
    for idx, r in enumerate(shifts):
        xsh[idx, pl.ds(0, span), :] = xx[pl.ds(r, span), :]
    rc = min(CONV_ROWS, tm)
    for c0 in range(0, CONV_CH, LANES):
        for r0 in range(0, tm, rc):
            acc = jnp.zeros((rc, LANES), F32)
            for k in range(CONV_WIDTH):
                start = off_c + k * rs
                r = start % SUBLANES
                if r == 0:
                    tap = xx[pl.ds(start + r0, rc), c0:c0 + LANES]
                else:
                    tap = xsh[shifts.index(r), pl.ds(start - r + r0, rc), c0:c0 + LANES]
                acc = acc + cw_ref[k:k + 1, c0:c0 + LANES] * tap
            cc[pl.ds(r0, rc), c0:c0 + LANES] = acc
    y = cc[...] + cb_ref[...]
    mu = jnp.mean(y, -1, keepdims=True)
    var = jnp.mean(jnp.square(y - mu), -1, keepdims=True)
    y = (y - mu) * lax.rsqrt(var + EPS) * lng_ref[...] + lnb_ref[...]
    c = y * jax.nn.sigmoid(y)
    x1 = (x_ref[...] + _bdot(o_ref[...], wout_ref[0:NSA_WIDTH, :])
          + _bdot(c, wout_ref[NSA_WIDTH:NSA_WIDTH + CONV_CH, :]))
    h = (x1 * lax.rsqrt(jnp.mean(x1 * x1, -1, keepdims=True) + EPS) * fg_ref[...]).astype(BF16)
    fw = D_FF // FF_SPLIT
    f = jnp.zeros((tm, D_MODEL), F32)
    for s in range(FF_SPLIT):
        c0 = s * fw
        gg[pl.ds(hpf, tm), c0:c0 + fw] = jnp.dot(h, wup_ref[:, c0:c0 + fw], preferred_element_type=F32)
        up = jnp.dot(h, wup_ref[:, D_FF + c0:D_FF + c0 + fw], preferred_element_type=F32)
        gc = fcb_ref[:, c0:c0 + fw]
        for k in range(FFN_CONV_WIDTH):
            gc = gc + fcw_ref[k:k + 1, c0:c0 + fw] * gg[pl.ds(off_f + k * rs, tm), c0:c0 + fw]
        act = gc * jax.nn.sigmoid(gc) * up
        f = f + jnp.dot(act.astype(BF16), wdn_ref[c0:c0 + fw, :], preferred_element_type=F32)
    y_ref[...] = x1 + f
    tail_ref[0] = gg[pl.ds(tm, hpf), :]
    xx[pl.ds(0, hpc), :] = xx[pl.ds(tm, hpc), :]
    gg[pl.ds(0, hpf), :] = gg[pl.ds(tm, hpf), :]


def _merge_ffn(x, o, u, cbuf, fbuf, cw, cb, lng, lnb, wout, fg, wup, fcw, fcb, wdn, tm, rs):
    t = x.shape[0]
    n_batch = cbuf.shape[0]
    nt = t // n_batch // tm
    hpc = cbuf.shape[1]
    hpf = fbuf.shape[1]
    n_shift = len(_conv_shifts(hpc, rs))
    xsh_shape = (n_shift, hpc + tm - SUBLANES, CONV_CH) if n_shift else (1, SUBLANES, LANES)
    tok = lambda b, i: (b * nt + i, 0)
    per_b = lambda b, i: (b, 0, 0)
    return pl.pallas_call(
        functools.partial(_merge_ffn_kernel, rs=rs),
        grid=(n_batch, nt),
        in_specs=[pl.BlockSpec((tm, D_MODEL), tok), pl.BlockSpec((tm, NSA_WIDTH), tok),
                  pl.BlockSpec((tm, CONV_CH), tok),
                  pl.BlockSpec((1, hpc, CONV_CH), per_b), pl.BlockSpec((1, hpf, D_FF), per_b),
                  _const_spec(cw.shape), _const_spec(cb.shape), _const_spec(lng.shape), _const_spec(lnb.shape),
                  _const_spec(wout.shape), _const_spec(fg.shape), _const_spec(wup.shape),
                  _const_spec(fcw.shape), _const_spec(fcb.shape), _const_spec(wdn.shape)],
        out_specs=[pl.BlockSpec((tm, D_MODEL), tok), pl.BlockSpec((1, hpf, D_FF), per_b)],
        out_shape=[jax.ShapeDtypeStruct((t, D_MODEL), F32), jax.ShapeDtypeStruct((n_batch, hpf, D_FF), F32)],
        scratch_shapes=[pltpu.VMEM((hpc + tm, CONV_CH), F32), pltpu.VMEM(xsh_shape, F32),
                        pltpu.VMEM((tm, CONV_CH), F32), pltpu.VMEM((hpf + tm, D_FF), F32)],
        compiler_params=pltpu.CompilerParams(dimension_semantics=("arbitrary", "arbitrary"),
                                             vmem_limit_bytes=VMEM_LIMIT),
        name="merge_ffn",
    )(x, o, u, cbuf, fbuf, cw, cb, lng, lnb, wout, fg, wup, fcw, fcb, wdn)


def _rope_tables(pos):
    half = HEAD_DIM // 2
    inv = ROPE_THETA ** (-jnp.arange(half, dtype=F32) / half)
    ang = pos.astype(F32)[:, None] * inv[None, :]
    cos, sin = jnp.cos(ang), jnp.sin(ang)
    reps = KV_W // HEAD_DIM
    return (jnp.tile(jnp.concatenate([cos, cos], -1), (1, reps)),
            jnp.tile(jnp.concatenate([-sin, sin], -1), (1, reps)))


def _head_sum_matrix(width):
    i = jnp.arange(width) // HEAD_DIM
    return (i[:, None] == i[None, :]).astype(BF16)


def _overlap_matrix(n_cmp_rows, n_blocks, n_lanes):
    c0 = jnp.arange(n_cmp_rows) * CMP_STRIDE
    s0 = jnp.arange(n_lanes) * SEL_BLOCK
    ov = jnp.clip(jnp.minimum(c0[:, None] + CMP_BLOCK, s0[None, :] + SEL_BLOCK)
                  - jnp.maximum(c0[:, None], s0[None, :]), 0, CMP_BLOCK).astype(F32) / CMP_BLOCK
    ov = jnp.where(jnp.arange(n_lanes)[None, :] < n_blocks, ov, 0.0)
    return ov.astype(BF16)


def _expand_matrix(n_rows, n_blocks, n_keys):
    n = jnp.arange(n_rows)[:, None]
    t = jnp.arange(n_keys)[None, :] // SEL_BLOCK
    return ((n == t) & (n < n_blocks)).astype(BF16)


def _chunk_order_matrix():
    r = jnp.arange(PAGE_SIZE)
    src = (r % PAGE_CHUNKS) * CMP_STRIDE + r // PAGE_CHUNKS
    return (src[:, None] == jnp.arange(PAGE_SIZE)[None, :]).astype(BF16)


def _gate_expand_matrix():
    col = jnp.arange(NSA_WIDTH)
    head = ((col % KV_W) // HEAD_DIM) * NSA_GROUP + col // KV_W
    j = jnp.arange(LANES)[None, :, None]
    br = jnp.arange(N_BRANCH)[:, None, None]
    return (j == head[None, None, :] * N_BRANCH + br).astype(BF16)


def _pair_order(a, axis):
    shape = a.shape
    a = a.reshape(shape[:axis] + (NSA_KV_HEADS, NSA_GROUP, HEAD_DIM) + shape[axis + 1:])
    return jnp.swapaxes(a, axis, axis + 1).reshape(shape)


def _slabs(a):
    t = jnp.moveaxis(a, -3, -1)
    return t.reshape((-1,) + t.shape[-2:])


def _compress_rows_weights(w1, w2, pos):
    half = CMP_STRIDE * HEAD_DIM
    eye = jnp.eye(NSA_KV_HEADS, dtype=F32)

    def spread(wpart):
        w3 = wpart.reshape(CMP_STRIDE, HEAD_DIM, CMP_HIDDEN)
        return jnp.einsum("tdh,kj->tkdjh", w3, eye).reshape(CHUNK_W, NSA_KV_HEADS * CMP_HIDDEN)

    wab = jnp.concatenate([spread(w1[:half]), spread(w1[half:])], axis=1).astype(BF16)
    w2bd = jnp.einsum("hd,kj->khjd", w2, eye).reshape(NSA_KV_HEADS * CMP_HIDDEN, KV_W).astype(BF16)

    def pos_row(p):
        row = jnp.tile(p[:, None, :], (1, NSA_KV_HEADS, 1)).reshape(1, CHUNK_W)
        return jnp.concatenate([row, jnp.zeros((SUBLANES - 1, CHUNK_W), F32)], axis=0)

    posab = jnp.concatenate([pos_row(pos[:CMP_STRIDE]), pos_row(pos[CMP_STRIDE:])], axis=0)
    return wab, posab, w2bd


def _compress_pages_weights(w1, w2, pos):
    half = CMP_STRIDE * HEAD_DIM
    wt = jnp.concatenate([w1[:half].reshape(CMP_STRIDE, HEAD_DIM, CMP_HIDDEN),
                          w1[half:].reshape(CMP_STRIDE, HEAD_DIM, CMP_HIDDEN)], axis=2).astype(BF16)
    pos8 = jnp.concatenate([pos.reshape(1, CMP_BLOCK * HEAD_DIM),
                            jnp.zeros((SUBLANES - 1, CMP_BLOCK * HEAD_DIM), F32)], axis=0)
    return wt, w1.astype(BF16), pos8, w2.astype(BF16)


def _tile_lanes(v, reps):
    return jnp.tile(v, reps)[None, :]


PROMPT_TM = 512
FFN_TM = 512


def kernel(x_prompt, x_sample, cache_k_cmp, cache_v_cmp, cache_k_slc, cache_v_slc, state_k_win, state_v_win,
           state_conv, state_ffn_conv, page_table, attn_norm, w_in, q_norm, k_norm, kc_norm, cmp_pos,
           cmp_w1_k, cmp_w2_k, cmp_w1_v, cmp_w2_v, conv_w, conv_b, conv_ln_g, conv_ln_b, w_out, ffn_norm,
           w_up, ffn_conv_w, ffn_conv_b, w_down):
    B, S = x_prompt.shape[:2]
    DB, DS = x_sample.shape[:2]
    depth = w_in.shape[0]
    n_pool = cache_k_cmp.shape[1]
    n_pages = page_table.shape[1]
    past = n_pages * PAGE_SIZE
    win_buf = state_k_win.shape[2]
    p_win = min(WINDOW, S)
    TS = DB * DS

    cos_p, sin_p = _rope_tables(jnp.arange(S))
    cos_s, sin_s = _rope_tables(jnp.repeat(past + jnp.arange(DS), DB))
    bdq = _head_sum_matrix(NSA_WIDTH)
    bdk = _head_sum_matrix(KV_W)
    ncp = S // CMP_STRIDE
    ncs = past // CMP_STRIDE
    nb_p = S // SEL_BLOCK
    nb_s = -(-(past + DS) // SEL_BLOCK)
    nl_s = _round_up(nb_s, LANES)
    chunk_blocks = SAMPLE_CHUNK_PAGES * PAGE_SIZE // SEL_BLOCK
    ovt_p = _overlap_matrix(ncp, nb_p, LANES).T
    oh_p = _expand_matrix(LANES, nb_p, S).T
    gx = _gate_expand_matrix()
    ov_s = _overlap_matrix(ncs, nb_s, nl_s)
    ex_s = _expand_matrix(chunk_blocks, chunk_blocks, chunk_blocks * SEL_BLOCK)
    perm = _chunk_order_matrix()
    pt_flat = page_table.reshape(-1).astype(jnp.int32)

    o_g = NSA_WIDTH + 6 * KV_W
    w_cat = jnp.concatenate([_pair_order(w_in[:, :, :NSA_WIDTH], 2), w_in[:, :, NSA_WIDTH:o_g],
                             jnp.pad(w_in[:, :, o_g:o_g + GATE_COLS], ((0, 0), (0, 0), (0, LANES - GATE_COLS))),
                             w_in[:, :, o_g + GATE_COLS:]], axis=2).astype(BF16)
    w_out_b = jnp.concatenate([_pair_order(w_out[:, :NSA_WIDTH], 1), w_out[:, NSA_WIDTH:]], axis=1).astype(BF16)
    w_up_b = w_up.astype(BF16)
    w_down_b = w_down.astype(BF16)
    conv_w_p = jnp.pad(conv_w, ((0, 0), (0, _round_up(CONV_WIDTH, SUBLANES) - CONV_WIDTH), (0, 0)))
    fcw_p = jnp.pad(ffn_conv_w, ((0, 0), (0, SUBLANES - FFN_CONV_WIDTH), (0, 0)))

    kc_pool, vc_pool, ks_pool, vs_pool = (_slabs(a) for a in (cache_k_cmp, cache_v_cmp, cache_k_slc, cache_v_slc))
    kw_states, vw_states = _slabs(state_k_win), _slabs(state_v_win)

    hpc_p = _round_up(CONV_WIDTH - 1, SUBLANES)
    hpf_p = _round_up(FFN_CONV_WIDTH - 1, SUBLANES)
    conv0 = jnp.zeros((B, hpc_p, CONV_CH), F32)
    ffn0 = jnp.zeros((B, hpf_p, D_FF), F32)

    y_p = x_prompt.reshape(B * S, D_MODEL)
    y_s = x_sample.transpose(1, 0, 2).reshape(TS, D_MODEL)

    def to_bm(a):
        return a.reshape(DS, DB, a.shape[-1]).transpose(1, 0, 2)

    def heads(a, lead):
        return a.reshape(lead + (NSA_KV_HEADS, HEAD_DIM))

    outs = [[] for _ in range(16)]
    for l in range(depth):
        qg = _tile_lanes(q_norm[l], NSA_HEADS)
        kg = jnp.concatenate([_tile_lanes(k_norm[l, i], NSA_KV_HEADS) for i in range(N_BRANCH)], axis=1)
        kcg2 = _tile_lanes(kc_norm[l], NSA_KV_HEADS)
        kcg = kc_norm[l][None]
        rw_k = _compress_rows_weights(cmp_w1_k[l], cmp_w2_k[l], cmp_pos[l])
        rw_v = _compress_rows_weights(cmp_w1_v[l], cmp_w2_v[l], cmp_pos[l])
        pw_k = _compress_pages_weights(cmp_w1_k[l], cmp_w2_k[l], cmp_pos[l])
        pw_v = _compress_pages_weights(cmp_w1_v[l], cmp_w2_v[l], cmp_pos[l])
        merge_w = (conv_w_p[l], conv_b[l][None], conv_ln_g[l][None], conv_ln_b[l][None], w_out_b[l],
                   ffn_norm[l][None], w_up_b[l], fcw_p[l], ffn_conv_b[l][None], w_down_b[l])
        pt_l = pt_flat + l * n_pool

        q, kc, vc, ks, vs, kw, vw, gt, u = _inproj(y_p, attn_norm[l][None], w_cat[l], qg, kg, cos_p, sin_p, bdq,
                                                   PROMPT_TM)
        kcmp = _compress_rows(kc, *rw_k, kcg2, bdk, B, True)
        vcmp = _compress_rows(vc, *rw_v, kcg2, bdk, B, False)
        o = _attn_prompt(q, gt, kcmp, vcmp, ks, vs, kw, vw, ovt_p, oh_p, gx, B, S)
        y_p, tail = _merge_ffn(y_p, o, u, conv0, ffn0, *merge_w, FFN_TM, 1)
        for idx, a in enumerate((kc, vc, ks, vs)):
            outs[idx].append(heads(a, (B, S)))
        outs[4].append(heads(kw, (B, S))[:, -p_win:])
        outs[5].append(heads(vw, (B, S))[:, -p_win:])
        outs[6].append(u.reshape(B, S, CONV_CH)[:, -(CONV_WIDTH - 1):])
        outs[7].append(tail[:, -(FFN_CONV_WIDTH - 1):])

        q, kc, vc, ks, vs, kw, vw, gt, u = _inproj(y_s, attn_norm[l][None], w_cat[l], qg, kg, cos_s, sin_s, bdq, TS)
        kcmp = _compress_pages(pt_l, kc_pool, perm, *pw_k, kcg, DB, True)
        vcmp = _compress_pages(pt_l, vc_pool, perm, *pw_v, kcg, DB, False)
        kc, vc, ks, vs, kw, vw, u_bm = (to_bm(a) for a in (kc, vc, ks, vs, kw, vw, u))
        o = _attn_sample(pt_l, to_bm(q), to_bm(gt), kcmp, vcmp, ks, vs, kw_states, vw_states, l * DB, kw, vw,
                         ov_s, ex_s, ks_pool, vs_pool, past)
        o_tm = o.transpose(1, 0, 2).reshape(TS, NSA_WIDTH)
        cbuf = state_conv[l].transpose(1, 0, 2).reshape(1, (CONV_WIDTH - 1) * DB, CONV_CH)
        fbuf = state_ffn_conv[l].transpose(1, 0, 2).reshape(1, (FFN_CONV_WIDTH - 1) * DB, D_FF)
        y_s, tail = _merge_ffn(y_s, o_tm, u, cbuf, fbuf, *merge_w, TS, DB)
        for idx, a in enumerate((kc, vc, ks, vs)):
            outs[8 + idx].append(heads(a, (DB, DS)))
        outs[12].append(jnp.concatenate([state_k_win[l], heads(kw, (DB, DS))], 1)[:, -win_buf:])
        outs[13].append(jnp.concatenate([state_v_win[l], heads(vw, (DB, DS))], 1)[:, -win_buf:])
        outs[14].append(jnp.concatenate([state_conv[l], u_bm], 1)[:, -(CONV_WIDTH - 1):])
        outs[15].append(tail.reshape(FFN_CONV_WIDTH - 1, DB, D_FF).transpose(1, 0, 2))

    y_prompt = y_p.reshape(B, S, D_MODEL)
    y_sample = y_s.reshape(DS, DB, D_MODEL).transpose(1, 0, 2)
    return (y_prompt, y_sample) + tuple(jnp.stack(o) for o in outs)
```

```python
import functools

import jax
import jax.numpy as jnp
from jax import lax
from jax.experimental import pallas as pl
from jax.experimental.pallas import tpu as pltpu

F32 = jnp.float32
BF16 = jnp.bfloat16

D_MODEL = 1024
HEAD_DIM = 64
NSA_HEADS = 8
NSA_KV_HEADS = 2
NSA_GROUP = NSA_HEADS // NSA_KV_HEADS
NSA_WIDTH = NSA_HEADS * HEAD_DIM
KV_W = NSA_KV_HEADS * HEAD_DIM
CONV_CH = D_MODEL - NSA_WIDTH
N_BRANCH = 3
CMP_BLOCK = 32
CMP_STRIDE = 16
CMP_HIDDEN = 128
SEL_BLOCK = 64
N_SELECT = 16
WINDOW = 512
CONV_WIDTH = 31
D_FF = 2816
FFN_CONV_WIDTH = 3
ROPE_THETA = 10000.0
EPS = 1e-6
Q_BLOCK = 128
GATE_COLS = NSA_HEADS * N_BRANCH
PAGE_SIZE = 128
SCALE = HEAD_DIM ** -0.5
NEG = -1e30

LANES = 128
SUBLANES = 8
VMEM_LIMIT = 52 * 1024 * 1024
CHUNK_W = CMP_STRIDE * KV_W
PAGE_CHUNKS = PAGE_SIZE // CMP_STRIDE
Z_COLS = NSA_WIDTH + 6 * KV_W + LANES + 2 * CONV_CH
NT_DIMS = (((1,), (1,)), ((), ()))

assert NSA_KV_HEADS == 2 and KV_W == LANES and PAGE_SIZE == LANES and CMP_BLOCK == 2 * CMP_STRIDE


def _round_up(x, m):
    return (x + m - 1) // m * m


def _bdot(a, b):
    return jnp.dot(a.astype(BF16), b.astype(BF16), preferred_element_type=F32)


def _bdot_nt(a, b):
    return lax.dot_general(a.astype(BF16), b.astype(BF16), NT_DIMS, preferred_element_type=F32)


def _split(a):
    hi = a.astype(BF16)
    return hi, (a - hi.astype(F32)).astype(BF16)


def _split_dot(a, b_bf16):
    hi, lo = _split(a)
    return (jnp.dot(hi, b_bf16, preferred_element_type=F32)
            + jnp.dot(lo, b_bf16, preferred_element_type=F32))


def _head_rms(x, bd, gain):
    ms = _split_dot(x * x, bd) * (1.0 / HEAD_DIM)
    return x * lax.rsqrt(ms + EPS) * gain


def _rope(y, cos, sin_signed):
    w = y.shape[-1]
    half = HEAD_DIM // 2
    fwd = pltpu.roll(y, w - half, 1)
    bwd = pltpu.roll(y, half, 1)
    lane = lax.broadcasted_iota(jnp.int32, y.shape, 1)
    swapped = jnp.where((lane & (HEAD_DIM - 1)) < half, fwd, bwd)
    return y * cos + swapped * sin_signed


def _masked_softmax(s, mask):
    s = jnp.where(mask, s, NEG)
    m = jnp.max(s, -1, keepdims=True)
    e = jnp.where(mask, jnp.exp(s - m), 0.0)
    return e / jnp.maximum(jnp.sum(e, -1, keepdims=True), 1e-30)


def _softmax_start(rows, width):
    return jnp.full((rows, 1), NEG, F32), jnp.zeros((rows, 1), F32), jnp.zeros((rows, width), F32)


def _softmax_update(state, s, mask, weigh):
    m_i, l_i, acc = state
    s = jnp.where(mask, s, NEG)
    m_new = jnp.maximum(m_i, jnp.max(s, -1, keepdims=True))
    alpha = jnp.exp(m_i - m_new)
    e = jnp.where(mask, jnp.exp(s - m_new), 0.0)
    return m_new, alpha * l_i + jnp.sum(e, -1, keepdims=True), alpha * acc + weigh(e)


def _softmax_finish(state):
    _, l_i, acc = state
    return acc / jnp.maximum(l_i, 1e-30)


def _const_spec(shape):
    nd = len(shape)
    return pl.BlockSpec(shape, lambda *_: (0,) * nd, pipeline_mode=pl.Buffered(1))


def _layer_spec(stacked, layer):
    return pl.BlockSpec((None,) + stacked.shape[1:], lambda *_: (layer, 0, 0), pipeline_mode=pl.Buffered(1))


def _inproj_kernel(x_ref, g_ref, w_ref, qg_ref, kg_ref, cos_ref, sin_ref, bd_ref,
                   q_ref, kc_ref, vc_ref, ks_ref, vs_ref, kw_ref, vw_ref, gt_ref, u_ref):
    x = x_ref[...]
    h = x * lax.rsqrt(jnp.mean(x * x, -1, keepdims=True) + EPS) * g_ref[...]
    z = jnp.dot(h.astype(BF16), w_ref[...], preferred_element_type=F32)
    cos = cos_ref[...]
    sin = sin_ref[...]
    bdq = bd_ref[...]
    bdk = bdq[:KV_W, :KV_W]
    nrep = NSA_WIDTH // KV_W
    q = _head_rms(z[:, :NSA_WIDTH], bdq, qg_ref[...])
    q_ref[...] = _rope(q, jnp.concatenate([cos] * nrep, axis=1), jnp.concatenate([sin] * nrep, axis=1))
    kg = kg_ref[...]
    o = NSA_WIDTH
    kc_ref[...] = _rope(_head_rms(z[:, o:o + KV_W], bdk, kg[:, 0:KV_W]), cos, sin)
    vc_ref[...] = z[:, o + KV_W:o + 2 * KV_W]
    ks_ref[...] = _rope(_head_rms(z[:, o + 2 * KV_W:o + 3 * KV_W], bdk, kg[:, KV_W:2 * KV_W]), cos, sin)
    vs_ref[...] = z[:, o + 3 * KV_W:o + 4 * KV_W]
    kw_ref[...] = _rope(_head_rms(z[:, o + 4 * KV_W:o + 5 * KV_W], bdk, kg[:, 2 * KV_W:3 * KV_W]), cos, sin)
    vw_ref[...] = z[:, o + 5 * KV_W:o + 6 * KV_W]
    o += 6 * KV_W
    gt_ref[...] = jax.nn.sigmoid(z[:, o:o + LANES])
    o += LANES
    u_ref[...] = z[:, o:o + CONV_CH] * jax.nn.sigmoid(z[:, o + CONV_CH:o + 2 * CONV_CH])


def _inproj(x, g, w, layer, qg, kg, cos, sin, bd, tm):
    t = x.shape[0]
    nt = t // tm
    npos = cos.shape[0] // tm
    row = lambda i: (i, 0)
    pos = lambda i: (i % npos, 0)
    widths = [NSA_WIDTH] + [KV_W] * 6 + [LANES, CONV_CH]
    return pl.pallas_call(
        _inproj_kernel,
        grid=(nt,),
        in_specs=[pl.BlockSpec((tm, D_MODEL), row), _const_spec((1, D_MODEL)), _layer_spec(w, layer),
                  _const_spec((1, NSA_WIDTH)), _const_spec((1, 3 * KV_W)),
                  pl.BlockSpec((tm, KV_W), pos), pl.BlockSpec((tm, KV_W), pos),
                  _const_spec((NSA_WIDTH, NSA_WIDTH))],
        out_specs=[pl.BlockSpec((tm, wd), row) for wd in widths],
        out_shape=[jax.ShapeDtypeStruct((t, wd), F32) for wd in widths],
        compiler_params=pltpu.CompilerParams(dimension_semantics=("arbitrary",), vmem_limit_bytes=VMEM_LIMIT),
        name="inproj",
    )(x, g, w, qg, kg, cos, sin, bd)


def _compress_rows_kernel(x_ref, wab_ref, pos_ref, w2_ref, g_ref, bd_ref, out_ref, *, norm):
    x = x_ref[0]
    nch = x.shape[0]
    hid2 = NSA_KV_HEADS * CMP_HIDDEN
    wab = wab_ref[...]
    z = jnp.dot(x.astype(BF16), wab, preferred_element_type=F32)
    pos = pos_ref[...].astype(BF16)
    bias = (jnp.dot(pos[0:SUBLANES], wab[:, :hid2], preferred_element_type=F32)
            + jnp.dot(pos[SUBLANES:], wab[:, hid2:], preferred_element_type=F32))[0:1]
    nxt = pltpu.roll(z[:, hid2:], nch - 1, 0)
    hid = jax.nn.gelu(z[:, :hid2] + nxt + bias)
    y = jnp.dot(hid.astype(BF16), w2_ref[...], preferred_element_type=F32)
    if norm:
        y = _head_rms(y, bd_ref[...], g_ref[...])
    rowi = lax.broadcasted_iota(jnp.int32, y.shape, 0)
    out_ref[0] = jnp.where(rowi < nch - 1, y, 0.0)


def _compress_rows(rows, wab, pos, w2, g, bd, n_batch, norm):
    x = rows.reshape(n_batch, -1, CHUNK_W)
    nch = x.shape[1]
    return pl.pallas_call(
        functools.partial(_compress_rows_kernel, norm=norm),
        grid=(n_batch,),
        in_specs=[pl.BlockSpec((1, nch, CHUNK_W), lambda b: (b, 0, 0)), _const_spec(wab.shape),
                  _const_spec(pos.shape), _const_spec(w2.shape), _const_spec(g.shape), _const_spec(bd.shape)],
        out_specs=pl.BlockSpec((1, nch, KV_W), lambda b: (b, 0, 0)),
        out_shape=jax.ShapeDtypeStruct((n_batch, nch, KV_W), F32),
        compiler_params=pltpu.CompilerParams(dimension_semantics=("arbitrary",), vmem_limit_bytes=VMEM_LIMIT),
        name="compress_rows",
    )(x, wab, pos, w2, g, bd)


def _page_copies(pt_ref, pool_ref, dst, sem, first, n_pages):
    copies = []
    for p in range(n_pages):
        page = pt_ref[first + p]
        for h in range(NSA_KV_HEADS):
            copies.append(pltpu.make_async_copy(pool_ref.at[page * NSA_KV_HEADS + h],
                                                dst.at[h, :, pl.ds(p * PAGE_SIZE, PAGE_SIZE)], sem))
    return copies


REGROUP_UNROLL = 16


def _compress_pages_kernel(pt_ref, pool_ref, perm_ref, wt_ref, w1_ref, pos_ref, w2_ref, g_ref, out_ref,
                           buf, xg, sem, *, n_pages, norm):
    b = pl.program_id(0)
    nb = pl.num_programs(0)

    def copies(batch, slot):
        return _page_copies(pt_ref, pool_ref, buf.at[slot], sem.at[slot], batch * n_pages, n_pages)

    @pl.when(b == 0)
    def _():
        for c in copies(0, 0):
            c.start()

    @pl.when(b + 1 < nb)
    def _():
        for c in copies(b + 1, (b + 1) % 2):
            c.start()

    slot = b % 2
    for c in copies(b, slot):
        c.wait()

    perm = perm_ref[...]

    def regroup(i, carry):
        for k in range(REGROUP_UNROLL):
            p = i * REGROUP_UNROLL + k
            t0 = pl.multiple_of(p * PAGE_SIZE, PAGE_SIZE)
            for h in range(NSA_KV_HEADS):
                slab = buf[slot, h, :, pl.ds(t0, PAGE_SIZE)].astype(BF16)
                xg[h, p] = lax.dot_general(perm, slab, NT_DIMS, preferred_element_type=F32)
        return carry

    lax.fori_loop(0, n_pages // REGROUP_UNROLL, regroup, 0)

    nch = n_pages * PAGE_CHUNKS
    acc = jnp.zeros((NSA_KV_HEADS * nch, 2 * CMP_HIDDEN), F32)
    for t in range(CMP_STRIDE):
        x = jnp.concatenate([xg[h, :, pl.ds(t * PAGE_CHUNKS, PAGE_CHUNKS), :].reshape(nch, HEAD_DIM)
                             for h in range(NSA_KV_HEADS)], axis=0)
        acc = acc + jnp.dot(x.astype(BF16), wt_ref[t], preferred_element_type=F32)
    bias = jnp.dot(pos_ref[...].astype(BF16), w1_ref[...], preferred_element_type=F32)[0:1]
    nxt = pltpu.roll(acc[:, CMP_HIDDEN:], NSA_KV_HEADS * nch - 1, 0)
    hid = jax.nn.gelu(acc[:, :CMP_HIDDEN] + nxt + bias)
    y = jnp.dot(hid.astype(BF16), w2_ref[...], preferred_element_type=F32)
    if norm:
        y = y * lax.rsqrt(jnp.mean(y * y, -1, keepdims=True) + EPS) * g_ref[...]
    rowi = lax.broadcasted_iota(jnp.int32, y.shape, 0)
    last = (rowi == nch - 1) | (rowi == NSA_KV_HEADS * nch - 1)
    out_ref[0] = jnp.where(last, 0.0, y)


def _compress_pages(page_table, pool, perm, wt, w1, pos, w2, g, n_batch, norm):
    n_pages = page_table.shape[0] // n_batch
    nch = n_pages * PAGE_CHUNKS
    full = lambda a: pl.BlockSpec(a.shape, lambda b, pt: (0,) * a.ndim)
    grid_spec = pltpu.PrefetchScalarGridSpec(
        num_scalar_prefetch=1,
        grid=(n_batch,),
        in_specs=[pl.BlockSpec(memory_space=pl.ANY), full(perm), full(wt), full(w1), full(pos), full(w2), full(g)],
        out_specs=pl.BlockSpec((1, NSA_KV_HEADS * nch, HEAD_DIM), lambda b, pt: (b, 0, 0)),
        scratch_shapes=[pltpu.VMEM((2, NSA_KV_HEADS, HEAD_DIM, n_pages * PAGE_SIZE), F32),
                        pltpu.VMEM((NSA_KV_HEADS, n_pages, PAGE_SIZE, HEAD_DIM), F32),
                        pltpu.SemaphoreType.DMA((2,))],
    )
    return pl.pallas_call(
        functools.partial(_compress_pages_kernel, n_pages=n_pages, norm=norm),
        grid_spec=grid_spec,
        out_shape=jax.ShapeDtypeStruct((n_batch, NSA_KV_HEADS * nch, HEAD_DIM), F32),
        compiler_params=pltpu.CompilerParams(dimension_semantics=("arbitrary",), vmem_limit_bytes=VMEM_LIMIT),
        name="compress_pages",
    )(page_table, pool, perm, wt, w1, pos, w2, g)


def _select_lanes(imp, cur, n_blocks):
    n = lax.broadcasted_iota(jnp.int32, (1, imp.shape[1]), 1)
    forced = (n == 0) | (n == cur) | (n == cur - 1)
    v = jnp.where(forced, jnp.inf, jnp.where(n <= cur, imp, -jnp.inf))
    rank = jnp.zeros(imp.shape, jnp.int32)
    for m in range(n_blocks):
        vm = v[:, m:m + 1]
        before = (n > m).astype(jnp.int32)
        rank = rank + jnp.where(vm > v, 1, jnp.where(vm == v, before, 0))
    return jnp.where(rank < min(N_SELECT, n_blocks), 1.0, 0.0).astype(BF16)


def _select_rows(imp_t, cur, n_blocks):
    n = lax.broadcasted_iota(jnp.int32, imp_t.shape, 0)
    forced = (n == 0) | (n == cur) | (n == cur - 1)
    v = jnp.where(forced, jnp.inf, jnp.where(n <= cur, imp_t, -jnp.inf))
    rank = jnp.zeros(imp_t.shape, jnp.int32)
    for m in range(n_blocks):
        vm = v[m:m + 1, :]
        rank = rank + jnp.where(vm > v, 1, jnp.where(vm == v, (n > m).astype(jnp.int32), 0))
    return jnp.where(rank < min(N_SELECT, n_blocks), 1.0, 0.0)


KEY_BLOCK = 512


def _attn_prompt_kernel(q_ref, gt_ref, kcmp_ref, vcmp_ref, ks_ref, vs_ref, kw_ref, vw_ref, ovt_ref, oh_ref, gx_ref,
                        o_ref, ksa, vsb, kwb, vwb, kcb, vcb, *, seq):
    i = pl.program_id(1)

    @pl.when(i == 0)
    def _():
        ksa[:, 0:KV_W] = ks_ref[0].astype(BF16)
        ksa[:, KV_W:2 * KV_W] = oh_ref[...]
        vsb[...] = vs_ref[0].astype(BF16)
        kwb[...] = kw_ref[0].astype(BF16)
        vwb[...] = vw_ref[0].astype(BF16)
        kcb[...] = kcmp_ref[0].astype(BF16)
        vcb[...] = vcmp_ref[0].astype(BF16)

    qs = i * Q_BLOCK
    q = q_ref[...] * SCALE
    rows = NSA_GROUP * Q_BLOCK
    lane = lax.broadcasted_iota(jnp.int32, (1, KV_W), 1)
    qpos4 = qs + (lax.broadcasted_iota(jnp.int32, (rows, 1), 0) & (Q_BLOCK - 1))
    qpos_l = qs + lax.broadcasted_iota(jnp.int32, (1, Q_BLOCK), 1)
    n_cmp = kcb.shape[0]
    n_blocks = seq // SEL_BLOCK
    win_keys = WINDOW + Q_BLOCK
    n_kb = (qs + Q_BLOCK + KEY_BLOCK - 1) // KEY_BLOCK
    w0 = pl.multiple_of(jnp.maximum(qs - WINDOW, 0), Q_BLOCK)
    o_c, o_w, q4a = [], [], []
    for h in range(NSA_KV_HEADS):
        in_head = (lane >= h * HEAD_DIM) & (lane < (h + 1) * HEAD_DIM)
        q4 = jnp.concatenate([jnp.where(in_head, q[:, g * KV_W:(g + 1) * KV_W], 0.0) for g in range(NSA_GROUP)],
                             axis=0).astype(BF16)
        s_c = lax.dot_general(q4, kcb[...], NT_DIMS, preferred_element_type=F32)
        c = lax.broadcasted_iota(jnp.int32, (1, n_cmp), 1)
        p_c = _masked_softmax(s_c, (c * CMP_STRIDE + (CMP_BLOCK - 1)) <= qpos4)
        o_c.append(jnp.dot(p_c.astype(BF16), vcb[...], preferred_element_type=F32))
        p_sum = p_c[0:Q_BLOCK]
        for g in range(1, NSA_GROUP):
            p_sum = p_sum + p_c[g * Q_BLOCK:(g + 1) * Q_BLOCK]
        hi, lo = _split(p_sum)
        ovt = ovt_ref[...]
        imp_t = (lax.dot_general(ovt, hi, NT_DIMS, preferred_element_type=F32)
                 + lax.dot_general(ovt, lo, NT_DIMS, preferred_element_type=F32))[0:n_blocks]
        sel_t = _select_rows(imp_t, qpos_l // SEL_BLOCK, n_blocks)
        bias_t = jnp.concatenate([jnp.where(sel_t > 0.5, 0.0, NEG),
                                  jnp.zeros((KV_W - n_blocks, Q_BLOCK), F32)], axis=0)
        bias = bias_t.T.astype(BF16)
        q4a.append(jnp.concatenate([q4, jnp.concatenate([bias] * NSA_GROUP, axis=0)], axis=1))
        s_w = lax.dot_general(q4, kwb[pl.ds(w0, win_keys), :], NT_DIMS, preferred_element_type=F32)
        dp = qpos4 - (w0 + lax.broadcasted_iota(jnp.int32, (1, win_keys), 1))
        s_w = jnp.where((dp >= 0) & (dp < WINDOW), s_w, NEG)
        e_w = jnp.exp(s_w - jnp.max(s_w, -1, keepdims=True))
        acc_w = jnp.dot(e_w.astype(BF16), vwb[pl.ds(w0, win_keys), :], preferred_element_type=F32)
        o_w.append(acc_w * (1.0 / jnp.sum(e_w, -1, keepdims=True)))

    def sel_block(j, carry, causal):
        k0 = pl.multiple_of(j * KEY_BLOCK, KEY_BLOCK)
        kb = ksa[pl.ds(k0, KEY_BLOCK), :]
        vb = vsb[pl.ds(k0, KEY_BLOCK), :]
        new = []
        for h in range(NSA_KV_HEADS):
            m_i, l_i, acc = carry[h]
            s = lax.dot_general(q4a[h], kb, NT_DIMS, preferred_element_type=F32)
            if causal:
                tok = k0 + lax.broadcasted_iota(jnp.int32, (1, KEY_BLOCK), 1)
                s = jnp.where(tok <= qpos4, s, NEG)
            m_new = jnp.maximum(m_i, jnp.max(s, -1, keepdims=True))
            alpha = jnp.exp(m_i - m_new)
            e = jnp.exp(s - m_new)
            l_new = alpha * l_i + jnp.sum(e, -1, keepdims=True)
            new.append((m_new, l_new, alpha * acc + jnp.dot(e.astype(BF16), vb, preferred_element_type=F32)))
        return tuple(new)

    carry = tuple(_softmax_start(rows, KV_W) for _ in range(NSA_KV_HEADS))
    carry = lax.fori_loop(0, n_kb - 1, functools.partial(sel_block, causal=False), carry)
    carry = sel_block(n_kb - 1, carry, True)
    o_s = [acc_s * (1.0 / l_s) for _, l_s, acc_s in carry]

    first = lane < HEAD_DIM
    ghi, glo = _split(gt_ref[...])
    out = jnp.zeros((Q_BLOCK, NSA_WIDTH), F32)
    for br, parts in enumerate((o_c, o_s, o_w)):
        gx = gx_ref[br]
        gate = jnp.dot(ghi, gx, preferred_element_type=F32) + jnp.dot(glo, gx, preferred_element_type=F32)
        merged = jnp.concatenate([jnp.where(first, parts[0][g * Q_BLOCK:(g + 1) * Q_BLOCK],
                                            parts[1][g * Q_BLOCK:(g + 1) * Q_BLOCK])
                                  for g in range(NSA_GROUP)], axis=1)
        out = out + gate * merged
    o_ref[...] = out


def _attn_prompt(q, gt, kcmp, vcmp, ks, vs, kw, vw, ovt, oh, gx, n_batch, seq):
    nq = seq // Q_BLOCK
    n_cmp = kcmp.shape[1]
    tok = lambda b, i: (b * nq + i, 0)
    per_b = lambda b, i: (b, 0, 0)
    kv3 = lambda a: a.reshape(n_batch, seq, KV_W)
    return pl.pallas_call(
        functools.partial(_attn_prompt_kernel, seq=seq),
        grid=(n_batch, nq),
        in_specs=[pl.BlockSpec((Q_BLOCK, NSA_WIDTH), tok), pl.BlockSpec((Q_BLOCK, LANES), tok),
                  pl.BlockSpec((1, n_cmp, KV_W), per_b), pl.BlockSpec((1, n_cmp, KV_W), per_b),
                  pl.BlockSpec((1, seq, KV_W), per_b), pl.BlockSpec((1, seq, KV_W), per_b),
                  pl.BlockSpec((1, seq, KV_W), per_b), pl.BlockSpec((1, seq, KV_W), per_b),
                  _const_spec(ovt.shape), _const_spec(oh.shape), _const_spec(gx.shape)],
        out_specs=pl.BlockSpec((Q_BLOCK, NSA_WIDTH), tok),
        out_shape=jax.ShapeDtypeStruct((n_batch * seq, NSA_WIDTH), F32),
        scratch_shapes=[pltpu.VMEM((seq, 2 * KV_W), BF16), pltpu.VMEM((seq, KV_W), BF16),
                        pltpu.VMEM((seq, KV_W), BF16), pltpu.VMEM((seq, KV_W), BF16),
                        pltpu.VMEM((n_cmp, KV_W), BF16), pltpu.VMEM((n_cmp, KV_W), BF16)],
        compiler_params=pltpu.CompilerParams(dimension_semantics=("arbitrary", "arbitrary"),
                                             vmem_limit_bytes=VMEM_LIMIT),
        name="attn_prompt",
    )(q, gt, kcmp, vcmp, kv3(ks), kv3(vs), kv3(kw), kv3(vw), ovt, oh, gx)


SAMPLE_CHUNK_PAGES = 16


def _attn_sample_kernel(pt_ref, q_ref, gt_ref, kcmp_ref, vcmp_ref, ksn_ref, vsn_ref, kws_ref, vws_ref,
                        kwn_ref, vwn_ref, ov_ref, ex_ref, kpool_ref, vpool_ref,
                        o_ref, kbuf, vbuf, ksem, vsem, *, n_pages, past):
    b = pl.program_id(0)
    nb = pl.num_programs(0)
    n_chunks = n_pages // SAMPLE_CHUNK_PAGES
    chunk_blocks = SAMPLE_CHUNK_PAGES * PAGE_SIZE // SEL_BLOCK
    assert n_chunks % 2 == 0

    def copies(batch, chunk):
        slot = chunk % 2
        first = batch * n_pages + chunk * SAMPLE_CHUNK_PAGES
        return (_page_copies(pt_ref, kpool_ref, kbuf.at[slot], ksem.at[slot], first, SAMPLE_CHUNK_PAGES)
                + _page_copies(pt_ref, vpool_ref, vbuf.at[slot], vsem.at[slot], first, SAMPLE_CHUNK_PAGES))

    @pl.when(b == 0)
    def _():
        for c in copies(0, 0):
            c.start()

    ds = q_ref.shape[1]
    q = q_ref[0] * SCALE
    gt = gt_ref[0]
    per_head = NSA_GROUP * ds
    heads = range(NSA_KV_HEADS)
    hs = [slice(h * HEAD_DIM, (h + 1) * HEAD_DIM) for h in heads]
    qh = [jnp.concatenate([q[:, g * KV_W + h * HEAD_DIM:g * KV_W + (h + 1) * HEAD_DIM] for g in range(NSA_GROUP)],
                          axis=0).astype(BF16) for h in heads]
    rows = NSA_KV_HEADS * per_head
    trow = lax.broadcasted_iota(jnp.int32, (rows, 1), 0) & (ds - 1)
    qpos = past + trow

    def scores_t(kt):
        return jnp.concatenate([jnp.dot(qh[h], kt[h].astype(BF16), preferred_element_type=F32) for h in heads],
                               axis=0)

    def scores(k):
        return jnp.concatenate([_bdot_nt(qh[h], k[h]) for h in heads], axis=0)

    def weigh_t(vt):
        return lambda e: jnp.concatenate([_bdot_nt(e[h * per_head:(h + 1) * per_head], vt[h]) for h in heads],
                                         axis=0)

    def weigh(v):
        return lambda e: jnp.concatenate([_bdot(e[h * per_head:(h + 1) * per_head], v[h]) for h in heads], axis=0)

    n_cmp = kcmp_ref.shape[1] // NSA_KV_HEADS
    kc = kcmp_ref[0]
    vc = vcmp_ref[0]
    cmp_rows = [slice(h * n_cmp, (h + 1) * n_cmp) for h in heads]
    c = lax.broadcasted_iota(jnp.int32, (1, n_cmp), 1)
    p_c = _masked_softmax(scores([kc[r] for r in cmp_rows]), (c * CMP_STRIDE + (CMP_BLOCK - 1)) <= qpos)
    o_c = weigh([vc[r] for r in cmp_rows])(p_c)
    sums = []
    for h in heads:
        acc = p_c[h * per_head:h * per_head + ds]
        for g in range(1, NSA_GROUP):
            acc = acc + p_c[h * per_head + g * ds:h * per_head + (g + 1) * ds]
        sums.append(acc)
    imp = _split_dot(jnp.concatenate(sums, axis=0), ov_ref[...])
    qpos_sel = past + (lax.broadcasted_iota(jnp.int32, (NSA_KV_HEADS * ds, 1), 0) & (ds - 1))
    n_blocks = -(-(past + ds) // SEL_BLOCK)
    sel = _select_lanes(imp, qpos_sel // SEL_BLOCK, n_blocks)
    sel = jnp.concatenate([sel[h * ds:(h + 1) * ds] for h in heads for _ in range(NSA_GROUP)], axis=0)

    state = _softmax_start(rows, HEAD_DIM)
    for ch in range(n_chunks):
        if ch + 1 < n_chunks:
            for cp in copies(b, ch + 1):
                cp.start()
        else:
            @pl.when(b + 1 < nb)
            def _():
                for cp in copies(b + 1, 0):
                    cp.start()
        for cp in copies(b, ch):
            cp.wait()
        slot = ch % 2
        selm = jnp.dot(sel[:, ch * chunk_blocks:(ch + 1) * chunk_blocks], ex_ref[...], preferred_element_type=F32)
        state = _softmax_update(state, scores_t([kbuf[slot, h] for h in heads]), selm > 0.5,
                                weigh_t([vbuf[slot, h] for h in heads]))
    new_block = past // SEL_BLOCK
    tok = lax.broadcasted_iota(jnp.int32, (1, ds), 1)
    picked = sel[:, new_block:new_block + 1].astype(F32) > 0.5
    ksn = ksn_ref[0]
    vsn = vsn_ref[0]
    state = _softmax_update(state, scores([ksn[:, s] for s in hs]), picked & (tok <= trow),
                            weigh([vsn[:, s] for s in hs]))
    o_s = _softmax_finish(state)

    nwin = kws_ref.shape[2]
    state = _softmax_start(rows, HEAD_DIM)
    dp = qpos - (past - nwin + lax.broadcasted_iota(jnp.int32, (1, nwin), 1))
    state = _softmax_update(state, scores_t([kws_ref[h] for h in heads]), (dp >= 0) & (dp < WINDOW),
                            weigh_t([vws_ref[h] for h in heads]))
    dp = trow - tok
    kwn = kwn_ref[0]
    vwn = vwn_ref[0]
    state = _softmax_update(state, scores([kwn[:, s] for s in hs]), (dp >= 0) & (dp < WINDOW),
                            weigh([vwn[:, s] for s in hs]))
    o_w = _softmax_finish(state)

    pieces = []
    for g in range(NSA_GROUP):
        for h in heads:
            j = (h * NSA_GROUP + g) * N_BRANCH
            r = slice(h * per_head + g * ds, h * per_head + (g + 1) * ds)
            pieces.append(gt[:, j:j + 1] * o_c[r] + gt[:, j + 1:j + 2] * o_s[r] + gt[:, j + 2:j + 3] * o_w[r])
    o_ref[0] = jnp.concatenate(pieces, axis=1)


def _attn_sample(page_table, q, gt, kcmp, vcmp, ksn, vsn, kws, vws, win_first, kwn, vwn, ov, ex, kpool, vpool, past):
    n_batch, ds = q.shape[:2]
    n_pages = page_table.shape[0] // n_batch
    chunk_tokens = SAMPLE_CHUNK_PAGES * PAGE_SIZE
    per_b = lambda b, pt: (b, 0, 0)
    blk = lambda a: pl.BlockSpec((1,) + a.shape[1:], per_b)
    win = lambda a: pl.BlockSpec((NSA_KV_HEADS,) + a.shape[1:], lambda b, pt: (win_first + b, 0, 0))
    full = lambda a: pl.BlockSpec(a.shape, lambda b, pt: (0,) * a.ndim)
    slab_buf = pltpu.VMEM((2, NSA_KV_HEADS, HEAD_DIM, chunk_tokens), F32)
    grid_spec = pltpu.PrefetchScalarGridSpec(
        num_scalar_prefetch=1,
        grid=(n_batch,),
        in_specs=[blk(q), blk(gt), blk(kcmp), blk(vcmp), blk(ksn), blk(vsn), win(kws), win(vws), blk(kwn), blk(vwn),
                  full(ov), full(ex),
                  pl.BlockSpec(memory_space=pl.ANY), pl.BlockSpec(memory_space=pl.ANY)],
        out_specs=pl.BlockSpec((1, ds, NSA_WIDTH), per_b),
        scratch_shapes=[slab_buf, slab_buf, pltpu.SemaphoreType.DMA((2,)), pltpu.SemaphoreType.DMA((2,))],
    )
    return pl.pallas_call(
        functools.partial(_attn_sample_kernel, n_pages=n_pages, past=past),
        grid_spec=grid_spec,
        out_shape=jax.ShapeDtypeStruct((n_batch, ds, NSA_WIDTH), F32),
        compiler_params=pltpu.CompilerParams(dimension_semantics=("arbitrary",), vmem_limit_bytes=VMEM_LIMIT),
        name="attn_sample",
    )(page_table, q, gt, kcmp, vcmp, ksn, vsn, kws, vws, kwn, vwn, ov, ex, kpool, vpool)


CONV_ROWS = 64
FF_SPLIT = 2


def _conv_shifts(hpc, rs):
    off = hpc - (CONV_WIDTH - 1) * rs
    return sorted({(off + k * rs) % SUBLANES for k in range(CONV_WIDTH)} - {0})


def _merge_ffn_kernel(x_ref, o_ref, u_ref, cbuf_ref, fbuf_ref, cw_ref, cb_ref, lng_ref, lnb_ref, wout_ref,
                      fg_ref, wup_ref, fcw_ref, fcb_ref, wdn_ref,
                      y_ref, tail_ref, xx, xsh, cc, gg, *, rs):
    t = pl.program_id(1)
    tm = x_ref.shape[0]
    hpc = xx.shape[0] - tm
    hpf = gg.shape[0] - tm
    off_c = hpc - (CONV_WIDTH - 1) * rs
    off_f = hpf - (FFN_CONV_WIDTH - 1) * rs

    @pl.when(t == 0)
    def _():
        xx[pl.ds(0, hpc), :] = cbuf_ref[0]
        gg[pl.ds(0, hpf), :] = fbuf_ref[0]

    xx[pl.ds(hpc, tm), :] = u_ref[...]
    shifts = _conv_shifts(hpc, rs)
    span = hpc + tm - SUBLANES
    for idx, r in enumerate(shifts):
        xsh[idx, pl.ds(0, span), :] = xx[pl.ds(r, span), :]
    rc = min(CONV_ROWS, tm)
    for c0 in range(0, CONV_CH, LANES):
        for r0 in range(0, tm, rc):
            acc = jnp.zeros((rc, LANES), F32)
            for k in range(CONV_WIDTH):
                start = off_c + k * rs
                r = start % SUBLANES
                if r == 0:
                    tap = xx[pl.ds(start + r0, rc), c0:c0 + LANES]
                else:
                    tap = xsh[shifts.index(r), pl.ds(start - r + r0, rc), c0:c0 + LANES]
                acc = acc + cw_ref[k:k + 1, c0:c0 + LANES] * tap
            cc[pl.ds(r0, rc), c0:c0 + LANES] = acc
    y = cc[...] + cb_ref[...]
    mu = jnp.mean(y, -1, keepdims=True)
    var = jnp.mean(jnp.square(y - mu), -1, keepdims=True)
    y = (y - mu) * lax.rsqrt(var + EPS) * lng_ref[...] + lnb_ref[...]
    c = y * jax.nn.sigmoid(y)
    x1 = (x_ref[...] + _bdot(o_ref[...], wout_ref[0:NSA_WIDTH, :])
          + _bdot(c, wout_ref[NSA_WIDTH:NSA_WIDTH + CONV_CH, :]))
    h = (x1 * lax.rsqrt(jnp.mean(x1 * x1, -1, keepdims=True) + EPS) * fg_ref[...]).astype(BF16)
    fw = D_FF // FF_SPLIT
    f = jnp.zeros((tm, D_MODEL), F32)
    for s in range(FF_SPLIT):
        c0 = s * fw
        gg[pl.ds(hpf, tm), c0:c0 + fw] = jnp.dot(h, wup_ref[:, c0:c0 + fw], preferred_element_type=F32)
        up = jnp.dot(h, wup_ref[:, D_FF + c0:D_FF + c0 + fw], preferred_element_type=F32)
        gc = fcb_ref[:, c0:c0 + fw]
        for k in range(FFN_CONV_WIDTH):
            gc = gc + fcw_ref[k:k + 1, c0:c0 + fw] * gg[pl.ds(off_f + k * rs, tm), c0:c0 + fw]
        act = gc * jax.nn.sigmoid(gc) * up
        f = f + jnp.dot(act.astype(BF16), wdn_ref[c0:c0 + fw, :], preferred_element_type=F32)
    y_ref[...] = x1 + f
    tail_ref[0] = gg[pl.ds(tm, hpf), :]
    xx[pl.ds(0, hpc), :] = xx[pl.ds(tm, hpc), :]
    gg[pl.ds(0, hpf), :] = gg[pl.ds(tm, hpf), :]


def _merge_ffn(x, o, u, cbuf, fbuf, cw, cb, lng, lnb, wout, fg, wup, fcw, fcb, wdn, layer, tm, rs):
    t = x.shape[0]
    n_batch = cbuf.shape[0]
    nt = t // n_batch // tm
    hpc = cbuf.shape[1]
    hpf = fbuf.shape[1]
    n_shift = len(_conv_shifts(hpc, rs))
    xsh_shape = (n_shift, hpc + tm - SUBLANES, CONV_CH) if n_shift else (1, SUBLANES, LANES)
    tok = lambda b, i: (b * nt + i, 0)
    per_b = lambda b, i: (b, 0, 0)
    return pl.pallas_call(
        functools.partial(_merge_ffn_kernel, rs=rs),
        grid=(n_batch, nt),
        in_specs=[pl.BlockSpec((tm, D_MODEL), tok), pl.BlockSpec((tm, NSA_WIDTH), tok),
                  pl.BlockSpec((tm, CONV_CH), tok),
                  pl.BlockSpec((1, hpc, CONV_CH), per_b), pl.BlockSpec((1, hpf, D_FF), per_b),
                  _const_spec(cw.shape), _const_spec(cb.shape), _const_spec(lng.shape), _const_spec(lnb.shape),
                  _layer_spec(wout, layer), _const_spec(fg.shape), _layer_spec(wup, layer),
                  _const_spec(fcw.shape), _const_spec(fcb.shape), _layer_spec(wdn, layer)],
        out_specs=[pl.BlockSpec((tm, D_MODEL), tok), pl.BlockSpec((1, hpf, D_FF), per_b)],
        out_shape=[jax.ShapeDtypeStruct((t, D_MODEL), F32), jax.ShapeDtypeStruct((n_batch, hpf, D_FF), F32)],
        scratch_shapes=[pltpu.VMEM((hpc + tm, CONV_CH), F32), pltpu.VMEM(xsh_shape, F32),
                        pltpu.VMEM((tm, CONV_CH), F32), pltpu.VMEM((hpf + tm, D_FF), F32)],
        compiler_params=pltpu.CompilerParams(dimension_semantics=("arbitrary", "arbitrary"),
                                             vmem_limit_bytes=VMEM_LIMIT),
        name="merge_ffn",
    )(x, o, u, cbuf, fbuf, cw, cb, lng, lnb, wout, fg, wup, fcw, fcb, wdn)


def _rope_tables(pos):
    half = HEAD_DIM // 2
    inv = ROPE_THETA ** (-jnp.arange(half, dtype=F32) / half)
    ang = pos.astype(F32)[:, None] * inv[None, :]
    cos, sin = jnp.cos(ang), jnp.sin(ang)
    reps = KV_W // HEAD_DIM
    return (jnp.tile(jnp.concatenate([cos, cos], -1), (1, reps)),
            jnp.tile(jnp.concatenate([-sin, sin], -1), (1, reps)))


def _head_sum_matrix(width):
    i = jnp.arange(width) // HEAD_DIM
    return (i[:, None] == i[None, :]).astype(BF16)


def _overlap_matrix(n_cmp_rows, n_blocks, n_lanes):
    c0 = jnp.arange(n_cmp_rows) * CMP_STRIDE
    s0 = jnp.arange(n_lanes) * SEL_BLOCK
    ov = jnp.clip(jnp.minimum(c0[:, None] + CMP_BLOCK, s0[None, :] + SEL_BLOCK)
                  - jnp.maximum(c0[:, None], s0[None, :]), 0, CMP_BLOCK).astype(F32) / CMP_BLOCK
    ov = jnp.where(jnp.arange(n_lanes)[None, :] < n_blocks, ov, 0.0)
    return ov.astype(BF16)


def _expand_matrix(n_rows, n_blocks, n_keys):
    n = jnp.arange(n_rows)[:, None]
    t = jnp.arange(n_keys)[None, :] // SEL_BLOCK
    return ((n == t) & (n < n_blocks)).astype(BF16)


def _chunk_order_matrix():
    r = jnp.arange(PAGE_SIZE)
    src = (r % PAGE_CHUNKS) * CMP_STRIDE + r // PAGE_CHUNKS
    return (src[:, None] == jnp.arange(PAGE_SIZE)[None, :]).astype(BF16)


def _gate_expand_matrix():
    col = jnp.arange(NSA_WIDTH)
    head = ((col % KV_W) // HEAD_DIM) * NSA_GROUP + col // KV_W
    j = jnp.arange(LANES)[None, :, None]
    br = jnp.arange(N_BRANCH)[:, None, None]
    return (j == head[None, None, :] * N_BRANCH + br).astype(BF16)


def _pair_order(a, axis):
    shape = a.shape
    a = a.reshape(shape[:axis] + (NSA_KV_HEADS, NSA_GROUP, HEAD_DIM) + shape[axis + 1:])
    return jnp.swapaxes(a, axis, axis + 1).reshape(shape)


def _slabs(a):
    t = jnp.moveaxis(a, -3, -1)
    return t.reshape((-1,) + t.shape[-2:])


def _compress_rows_weights(w1, w2, pos):
    half = CMP_STRIDE * HEAD_DIM
    eye = jnp.eye(NSA_KV_HEADS, dtype=F32)

    def spread(wpart):
        w3 = wpart.reshape(CMP_STRIDE, HEAD_DIM, CMP_HIDDEN)
        return jnp.einsum("tdh,kj->tkdjh", w3, eye).reshape(CHUNK_W, NSA_KV_HEADS * CMP_HIDDEN)

    wab = jnp.concatenate([spread(w1[:half]), spread(w1[half:])], axis=1).astype(BF16)
    w2bd = jnp.einsum("hd,kj->khjd", w2, eye).reshape(NSA_KV_HEADS * CMP_HIDDEN, KV_W).astype(BF16)

    def pos_row(p):
        row = jnp.tile(p[:, None, :], (1, NSA_KV_HEADS, 1)).reshape(1, CHUNK_W)
        return jnp.concatenate([row, jnp.zeros((SUBLANES - 1, CHUNK_W), F32)], axis=0)

    posab = jnp.concatenate([pos_row(pos[:CMP_STRIDE]), pos_row(pos[CMP_STRIDE:])], axis=0)
    return wab, posab, w2bd


def _compress_pages_weights(w1, w2, pos):
    half = CMP_STRIDE * HEAD_DIM
    wt = jnp.concatenate([w1[:half].reshape(CMP_STRIDE, HEAD_DIM, CMP_HIDDEN),
                          w1[half:].reshape(CMP_STRIDE, HEAD_DIM, CMP_HIDDEN)], axis=2).astype(BF16)
    pos8 = jnp.concatenate([pos.reshape(1, CMP_BLOCK * HEAD_DIM),
                            jnp.zeros((SUBLANES - 1, CMP_BLOCK * HEAD_DIM), F32)], axis=0)
    return wt, w1.astype(BF16), pos8, w2.astype(BF16)


def _tile_lanes(v, reps):
    return jnp.tile(v, reps)[None, :]


PROMPT_TM = 512
FFN_TM = 512


def kernel(x_prompt, x_sample, cache_k_cmp, cache_v_cmp, cache_k_slc, cache_v_slc, state_k_win, state_v_win,
           state_conv, state_ffn_conv, page_table, attn_norm, w_in, q_norm, k_norm, kc_norm, cmp_pos,
           cmp_w1_k, cmp_w2_k, cmp_w1_v, cmp_w2_v, conv_w, conv_b, conv_ln_g, conv_ln_b, w_out, ffn_norm,
           w_up, ffn_conv_w, ffn_conv_b, w_down):
    B, S = x_prompt.shape[:2]
    DB, DS = x_sample.shape[:2]
    depth = w_in.shape[0]
    n_pool = cache_k_cmp.shape[1]
    n_pages = page_table.shape[1]
    past = n_pages * PAGE_SIZE
    win_buf = state_k_win.shape[2]
    p_win = min(WINDOW, S)
    TS = DB * DS

    cos_p, sin_p = _rope_tables(jnp.arange(S))
    cos_s, sin_s = _rope_tables(jnp.repeat(past + jnp.arange(DS), DB))
    bdq = _head_sum_matrix(NSA_WIDTH)
    bdk = _head_sum_matrix(KV_W)
    ncp = S // CMP_STRIDE
    ncs = past // CMP_STRIDE
    nb_p = S // SEL_BLOCK
    nb_s = -(-(past + DS) // SEL_BLOCK)
    nl_s = _round_up(nb_s, LANES)
    chunk_blocks = SAMPLE_CHUNK_PAGES * PAGE_SIZE // SEL_BLOCK
    ovt_p = _overlap_matrix(ncp, nb_p, LANES).T
    oh_p = _expand_matrix(LANES, nb_p, S).T
    gx = _gate_expand_matrix()
    ov_s = _overlap_matrix(ncs, nb_s, nl_s)
    ex_s = _expand_matrix(chunk_blocks, chunk_blocks, chunk_blocks * SEL_BLOCK)
    perm = _chunk_order_matrix()
    pt_flat = page_table.reshape(-1).astype(jnp.int32)

    o_g = NSA_WIDTH + 6 * KV_W
    w_cat = jnp.concatenate([_pair_order(w_in[:, :, :NSA_WIDTH], 2), w_in[:, :, NSA_WIDTH:o_g],
                             jnp.pad(w_in[:, :, o_g:o_g + GATE_COLS], ((0, 0), (0, 0), (0, LANES - GATE_COLS))),
                             w_in[:, :, o_g + GATE_COLS:]], axis=2).astype(BF16)
    w_out_b = jnp.concatenate([_pair_order(w_out[:, :NSA_WIDTH], 1), w_out[:, NSA_WIDTH:]], axis=1).astype(BF16)
    w_up_b = w_up.astype(BF16)
    w_down_b = w_down.astype(BF16)
    conv_w_p = jnp.pad(conv_w, ((0, 0), (0, _round_up(CONV_WIDTH, SUBLANES) - CONV_WIDTH), (0, 0)))
    fcw_p = jnp.pad(ffn_conv_w, ((0, 0), (0, SUBLANES - FFN_CONV_WIDTH), (0, 0)))

    kc_pool, vc_pool, ks_pool, vs_pool = (_slabs(a) for a in (cache_k_cmp, cache_v_cmp, cache_k_slc, cache_v_slc))
    kw_states, vw_states = _slabs(state_k_win), _slabs(state_v_win)

    hpc_p = _round_up(CONV_WIDTH - 1, SUBLANES)
    hpf_p = _round_up(FFN_CONV_WIDTH - 1, SUBLANES)
    conv0 = jnp.zeros((B, hpc_p, CONV_CH), F32)
    ffn0 = jnp.zeros((B, hpf_p, D_FF), F32)

    y_p = x_prompt.reshape(B * S, D_MODEL)
    y_s = x_sample.transpose(1, 0, 2).reshape(TS, D_MODEL)

    def to_bm(a):
        return a.reshape(DS, DB, a.shape[-1]).transpose(1, 0, 2)

    def heads(a, lead):
        return a.reshape(lead + (NSA_KV_HEADS, HEAD_DIM))

    outs = [[] for _ in range(16)]
    for l in range(depth):
        qg = _tile_lanes(q_norm[l], NSA_HEADS)
        kg = jnp.concatenate([_tile_lanes(k_norm[l, i], NSA_KV_HEADS) for i in range(N_BRANCH)], axis=1)
        kcg2 = _tile_lanes(kc_norm[l], NSA_KV_HEADS)
        kcg = kc_norm[l][None]
        rw_k = _compress_rows_weights(cmp_w1_k[l], cmp_w2_k[l], cmp_pos[l])
        rw_v = _compress_rows_weights(cmp_w1_v[l], cmp_w2_v[l], cmp_pos[l])
        pw_k = _compress_pages_weights(cmp_w1_k[l], cmp_w2_k[l], cmp_pos[l])
        pw_v = _compress_pages_weights(cmp_w1_v[l], cmp_w2_v[l], cmp_pos[l])
        merge_w = (conv_w_p[l], conv_b[l][None], conv_ln_g[l][None], conv_ln_b[l][None], w_out_b,
                   ffn_norm[l][None], w_up_b, fcw_p[l], ffn_conv_b[l][None], w_down_b, l)
        pt_l = pt_flat + l * n_pool

        q, kc, vc, ks, vs, kw, vw, gt, u = _inproj(y_p, attn_norm[l][None], w_cat, l, qg, kg, cos_p, sin_p, bdq,
                                                   PROMPT_TM)
        kcmp = _compress_rows(kc, *rw_k, kcg2, bdk, B, True)
        vcmp = _compress_rows(vc, *rw_v, kcg2, bdk, B, False)
        o = _attn_prompt(q, gt, kcmp, vcmp, ks, vs, kw, vw, ovt_p, oh_p, gx, B, S)
        y_p, tail = _merge_ffn(y_p, o, u, conv0, ffn0, *merge_w, FFN_TM, 1)
        for idx, a in enumerate((kc, vc, ks, vs)):
            outs[idx].append(heads(a, (B, S)))
        outs[4].append(heads(kw, (B, S))[:, -p_win:])
        outs[5].append(heads(vw, (B, S))[:, -p_win:])
        outs[6].append(u.reshape(B, S, CONV_CH)[:, -(CONV_WIDTH - 1):])
        outs[7].append(tail[:, -(FFN_CONV_WIDTH - 1):])

        q, kc, vc, ks, vs, kw, vw, gt, u = _inproj(y_s, attn_norm[l][None], w_cat, l, qg, kg, cos_s, sin_s, bdq, TS)
        kcmp = _compress_pages(pt_l, kc_pool, perm, *pw_k, kcg, DB, True)
        vcmp = _compress_pages(pt_l, vc_pool, perm, *pw_v, kcg, DB, False)
        kc, vc, ks, vs, kw, vw, u_bm = (to_bm(a) for a in (kc, vc, ks, vs, kw, vw, u))
        o = _attn_sample(pt_l, to_bm(q), to_bm(gt), kcmp, vcmp, ks, vs, kw_states, vw_states, l * DB, kw, vw,
                         ov_s, ex_s, ks_pool, vs_pool, past)
        o_tm = o.transpose(1, 0, 2).reshape(TS, NSA_WIDTH)
        cbuf = state_conv[l].transpose(1, 0, 2).reshape(1, (CONV_WIDTH - 1) * DB, CONV_CH)
        fbuf = state_ffn_conv[l].transpose(1, 0, 2).reshape(1, (FFN_CONV_WIDTH - 1) * DB, D_FF)
        y_s, tail = _merge_ffn(y_s, o_tm, u, cbuf, fbuf, *merge_w, TS, DB)
        for idx, a in enumerate((kc, vc, ks, vs)):
            outs[8 + idx].append(heads(a, (DB, DS)))
        outs[12].append(jnp.concatenate([state_k_win[l], heads(kw, (DB, DS))], 1)[:, -win_buf:])
        outs[13].append(jnp.concatenate([state_v_win[l], heads(vw, (DB, DS))], 1)[:, -win_buf:])
        outs[14].append(jnp.concatenate([state_conv[l], u_bm], 1)[:, -(CONV_WIDTH - 1):])
        outs[15].append(tail.reshape(FFN_CONV_WIDTH - 1, DB, D_FF).transpose(1, 0, 2))

    y_prompt = y_p.reshape(B, S, D_MODEL)
    y_sample = y_s.reshape(DS, DB, D_MODEL).transpose(1, 0, 2)
    return (y_prompt, y_sample) + tuple(jnp.stack(o) for o in outs)
```
